```python
import math
import jax, jax.numpy as jnp
from jax import lax
import numpy as np

D_MODEL = 2048
BATCH = 2
SEQ = 4096
DEPTH = 2

HEAD_DIM = 128
N_HEADS = D_MODEL // HEAD_DIM
NA_HEADS = N_HEADS // 2
DA_HEADS = N_HEADS - NA_HEADS
DA_QK_DIM = HEAD_DIM // 2
GRID_W = 64
NA_KH = 8
NA_KW = 16
ROPE_THETA = 10000.0
Q_BLOCK = 128
IN_COLS = 3 * (NA_HEADS + DA_HEADS) * HEAD_DIM
MIX_WIDTH = (NA_HEADS + DA_HEADS) * HEAD_DIM
LRU_WIDTH = D_MODEL
LRU_BLOCKS = 16
LRU_BLOCK_DIM = LRU_WIDTH // LRU_BLOCKS
CONV_WIDTH = 4
CONV_PAD_LEFT = (CONV_WIDTH - 1) // 2
CONV_PAD_RIGHT = CONV_WIDTH - 1 - CONV_PAD_LEFT
LRU_C = 8.0
D_FF = 7 * D_MODEL // 2
N_EXPERTS = 8
TOP_K = 2
D_FF_EXPERT = 7 * D_MODEL // 2
N_EVEN = (DEPTH + 1) // 2
N_ODD = DEPTH // 2
EPS = 1e-6
NEG_INF = -1e30

kernel_name = "hybrid_natten_diffattn_rglru_moe_encoder"


def rms_norm(x, g):
    xf = x.astype(jnp.float32)
    y = xf * lax.rsqrt(jnp.mean(xf * xf, axis=-1, keepdims=True) + EPS)
    return (y * g.astype(jnp.float32)).astype(x.dtype)


def swiglu(h, wg, wu, wd):
    return (jax.nn.silu(h @ wg) * (h @ wu)) @ wd


def rope_tables(seq, dim):
    inv = 1.0 / (ROPE_THETA ** (jnp.arange(0, dim, 2, dtype=jnp.float32) / dim))
    ang = jnp.arange(seq, dtype=jnp.float32)[:, None] * inv[None, :]
    return jnp.cos(ang), jnp.sin(ang)


def apply_rope(x, cos, sin):
    x1, x2 = jnp.split(x, 2, axis=-1)
    c, s = cos.astype(x.dtype), sin.astype(x.dtype)
    return jnp.concatenate([x1 * c - x2 * s, x2 * c + x1 * s], axis=-1)


def neighbourhood_attention(q, k, v, rpb):
    b, s, h, d = q.shape
    rows = s // GRID_W
    kh = min(NA_KH, rows)
    kw = min(NA_KW, GRID_W)
    r = jnp.arange(rows)
    r_start = jnp.clip(r - kh // 2, 0, rows - kh)
    row_idx = r_start[:, None] + jnp.arange(kh)[None, :]
    row_off = row_idx - r[:, None]
    c = jnp.arange(GRID_W)
    c_start = jnp.clip(c - kw // 2, 0, GRID_W - kw)
    col_in = (c[None, :] >= c_start[:, None]) & (c[None, :] < c_start[:, None] + kw)
    col_off = jnp.clip(c[None, :] - c[:, None], -(NA_KW - 1), NA_KW - 1)
    qg = q.reshape(b, rows, GRID_W, h, d)
    kg = k.reshape(b, rows, GRID_W, h, d)[:, row_idx]
    vg = v.reshape(b, rows, GRID_W, h, d)[:, row_idx]
    scores = jnp.einsum('brqhd,brikhd->bhrqik', qg, kg,
                        preferred_element_type=jnp.float32) * (d ** -0.5)
    bias = rpb[:, (row_off + NA_KH - 1)[:, None, :, None],
               (col_off + NA_KW - 1)[None, :, None, :]].astype(jnp.float32)
    scores = jnp.where(col_in[:, None, :], scores + bias[None], NEG_INF)
    p = jax.nn.softmax(scores.reshape(b, h, rows, GRID_W, kh * GRID_W), axis=-1)
    p = p.reshape(b, h, rows, GRID_W, kh, GRID_W).astype(v.dtype)
    out = jnp.einsum('bhrqik,brikhd->brqhd', p, vg)
    return out.reshape(b, s, h * d)


def differential_attention(q, k, v, lq1, lk1, lq2, lk2, subln, lambda_init):
    b, s, h, _, dq = q.shape
    dv = v.shape[-1]
    cos, sin = rope_tables(s, dq)
    cos, sin = cos[:, None, None, :], sin[:, None, None, :]
    q = apply_rope(q, cos, sin)
    k = apply_rope(k, cos, sin)
    lam = (jnp.exp(jnp.sum((lq1 * lk1).astype(jnp.float32)))
           - jnp.exp(jnp.sum((lq2 * lk2).astype(jnp.float32))) + lambda_init)
    scale = dq ** -0.5
    n_blk = s // Q_BLOCK
    qb = q.reshape(b, n_blk, Q_BLOCK, h, 2, dq).transpose(1, 0, 2, 3, 4, 5)

    def block(q_blk):
        sc = jnp.einsum('bqhcd,bkhcd->bchqk', q_blk, k,
                        preferred_element_type=jnp.float32) * scale
        p = jax.nn.softmax(sc, axis=-1)
        attn = (p[:, 0] - lam * p[:, 1]).astype(v.dtype)
        return jnp.einsum('bhqk,bkhd->bqhd', attn, v)

    o = lax.map(block, qb)
    o = o.transpose(1, 0, 2, 3, 4).reshape(b, s, h, dv)
    o = rms_norm(o, subln) * (1.0 - lambda_init)
    return o.reshape(b, s, h * dv)


def _lin_combine(left, right):
    a1, b1 = left
    a2, b2 = right
    return a1 * a2, a2 * b1 + b2


def rg_lru(u, w_a, b_a, w_x, b_x, a_param, reverse):
    b, s, _ = u.shape
    ub = u.reshape(b, s, LRU_BLOCKS, LRU_BLOCK_DIM)
    gate_a = jax.nn.sigmoid(jnp.einsum('bsni,nij->bsnj', ub, w_a) + b_a.reshape(LRU_BLOCKS, LRU_BLOCK_DIM))
    gate_x = jax.nn.sigmoid(jnp.einsum('bsni,nij->bsnj', ub, w_x) + b_x.reshape(LRU_BLOCKS, LRU_BLOCK_DIM))
    gate_a = gate_a.reshape(b, s, LRU_WIDTH).astype(jnp.float32)
    gate_x = gate_x.reshape(b, s, LRU_WIDTH)
    log_a = -LRU_C * gate_a * jax.nn.softplus(-a_param.astype(jnp.float32))
    a = jnp.exp(log_a)
    mult = jnp.sqrt(-jnp.expm1(2.0 * log_a))
    bt = mult * (gate_x * u).astype(jnp.float32)
    _, hs = lax.associative_scan(_lin_combine, (a, bt), axis=1, reverse=reverse)
    return hs.astype(u.dtype)


def moe_swiglu(h, router, wg, wu, wd):
    b, s, d = h.shape
    t = h.reshape(b * s, d)
    logits = (t @ router).astype(jnp.float32)
    top_vals, top_idx = lax.top_k(logits, TOP_K)
    gates = jax.nn.softmax(top_vals, axis=-1)
    combine = jnp.sum(jax.nn.one_hot(top_idx, N_EXPERTS, dtype=jnp.float32) * gates[..., None], axis=1)
    combine = combine.astype(t.dtype)
    out = jnp.zeros_like(t)
    for e in range(N_EXPERTS):
        out = out + combine[:, e:e + 1] * swiglu(t, wg[e], wu[e], wd[e])
    return out.reshape(b, s, d)


def setup_inputs(seed: int = 0) -> dict:
    key = jax.random.key(seed)
    ks = iter(jax.random.split(key, 40))
    nrm = lambda shape, scale: jax.random.normal(next(ks), shape, jnp.float32) * scale
    gain = lambda shape: 1.0 + nrm(shape, 0.01)
    NE, NO, D = N_EVEN, N_ODD, D_MODEL
    a_init = jax.random.uniform(next(ks), (NO, 2, LRU_WIDTH), jnp.float32, 0.9, 0.999)
    return {
        "x": nrm((BATCH, SEQ, D), 1.0),
        "ev_mix_norm": gain((NE, D)),
        "ev_w_in": nrm((NE, D, IN_COLS), D ** -0.5),
        "ev_na_rpb": nrm((NE, NA_HEADS, 2 * NA_KH - 1, 2 * NA_KW - 1), 0.1),
        "ev_da_lambda_q1": nrm((NE, DA_QK_DIM), 0.1),
        "ev_da_lambda_k1": nrm((NE, DA_QK_DIM), 0.1),
        "ev_da_lambda_q2": nrm((NE, DA_QK_DIM), 0.1),
        "ev_da_lambda_k2": nrm((NE, DA_QK_DIM), 0.1),
        "ev_da_subln": gain((NE, HEAD_DIM)),
        "ev_w_out": nrm((NE, MIX_WIDTH, D), MIX_WIDTH ** -0.5),
        "ev_ffn_norm": gain((NE, D)),
        "ev_ffn_w_gate": nrm((NE, D, D_FF), D ** -0.5),
        "ev_ffn_w_up": nrm((NE, D, D_FF), D ** -0.5),
        "ev_ffn_w_down": nrm((NE, D_FF, D), D_FF ** -0.5),
        "od_mix_norm": gain((NO, D)),
        "od_w_in": nrm((NO, D, 2 * LRU_WIDTH), D ** -0.5),
        "od_conv_w": nrm((NO, CONV_WIDTH, LRU_WIDTH), CONV_WIDTH ** -0.5),
        "od_conv_b": nrm((NO, LRU_WIDTH), 0.01),
        "od_lru_w_a": nrm((NO, 2, LRU_BLOCKS, LRU_BLOCK_DIM, LRU_BLOCK_DIM), LRU_BLOCK_DIM ** -0.5),
        "od_lru_b_a": nrm((NO, 2, LRU_WIDTH), 0.1),
        "od_lru_w_x": nrm((NO, 2, LRU_BLOCKS, LRU_BLOCK_DIM, LRU_BLOCK_DIM), LRU_BLOCK_DIM ** -0.5),
        "od_lru_b_x": nrm((NO, 2, LRU_WIDTH), 0.1),
        "od_lru_a_param": jnp.log(a_init) - jnp.log1p(-a_init),
        "od_w_out": nrm((NO, LRU_WIDTH, D), LRU_WIDTH ** -0.5),
        "od_ffn_norm": gain((NO, D)),
        "od_router": nrm((NO, D, N_EXPERTS), D ** -0.5),
        "od_moe_w_gate": nrm((NO, N_EXPERTS, D, D_FF_EXPERT), D ** -0.5),
        "od_moe_w_up": nrm((NO, N_EXPERTS, D, D_FF_EXPERT), D ** -0.5),
        "od_moe_w_down": nrm((NO, N_EXPERTS, D_FF_EXPERT, D), D_FF_EXPERT ** -0.5),
        "final_norm": gain((D,)),
    }


def reference(x, ev_mix_norm, ev_w_in, ev_na_rpb, ev_da_lambda_q1, ev_da_lambda_k1,
              ev_da_lambda_q2, ev_da_lambda_k2, ev_da_subln, ev_w_out, ev_ffn_norm,
              ev_ffn_w_gate, ev_ffn_w_up, ev_ffn_w_down, od_mix_norm, od_w_in, od_conv_w,
              od_conv_b, od_lru_w_a, od_lru_b_a, od_lru_w_x, od_lru_b_x, od_lru_a_param,
              od_w_out, od_ffn_norm, od_router, od_moe_w_gate, od_moe_w_up, od_moe_w_down,
              final_norm):
    b, s, _ = x.shape
    na_w = NA_HEADS * HEAD_DIM
    da_w = DA_HEADS * HEAD_DIM
    split_at = (na_w, 2 * na_w, 3 * na_w, 3 * na_w + da_w, 3 * na_w + 2 * da_w)
    for layer in range(DEPTH):
        if layer % 2 == 0:
            i = layer // 2
            lambda_init = 0.8 - 0.6 * math.exp(-0.3 * layer)
            h = rms_norm(x, ev_mix_norm[i])
            proj = h @ ev_w_in[i]
            na_q, na_k, na_v, da_q, da_k, da_v = jnp.split(proj, split_at, axis=-1)
            na_out = neighbourhood_attention(
                na_q.reshape(b, s, NA_HEADS, HEAD_DIM), na_k.reshape(b, s, NA_HEADS, HEAD_DIM),
                na_v.reshape(b, s, NA_HEADS, HEAD_DIM), ev_na_rpb[i])
            da_out = differential_attention(
                da_q.reshape(b, s, DA_HEADS, 2, DA_QK_DIM), da_k.reshape(b, s, DA_HEADS, 2, DA_QK_DIM),
                da_v.reshape(b, s, DA_HEADS, HEAD_DIM), ev_da_lambda_q1[i], ev_da_lambda_k1[i],
                ev_da_lambda_q2[i], ev_da_lambda_k2[i], ev_da_subln[i], lambda_init)
            x = x + jnp.concatenate([na_out, da_out], axis=-1) @ ev_w_out[i]
            h = rms_norm(x, ev_ffn_norm[i])
            x = x + swiglu(h, ev_ffn_w_gate[i], ev_ffn_w_up[i], ev_ffn_w_down[i])
        else:
            i = layer // 2
            h = rms_norm(x, od_mix_norm[i])
            proj = h @ od_w_in[i]
            gate_branch, rec_branch = jnp.split(proj, 2, axis=-1)
            y = jax.nn.gelu(gate_branch)
            u = lax.conv_general_dilated(
                rec_branch, od_conv_w[i][:, None, :], window_strides=(1,),
                padding=[(CONV_PAD_LEFT, CONV_PAD_RIGHT)],
                dimension_numbers=('NWC', 'WIO', 'NWC'),
                feature_group_count=LRU_WIDTH) + od_conv_b[i]
            h_fwd = rg_lru(u, od_lru_w_a[i, 0], od_lru_b_a[i, 0], od_lru_w_x[i, 0],
                           od_lru_b_x[i, 0], od_lru_a_param[i, 0], reverse=False)
            h_bwd = rg_lru(u, od_lru_w_a[i, 1], od_lru_b_a[i, 1], od_lru_w_x[i, 1],
                           od_lru_b_x[i, 1], od_lru_a_param[i, 1], reverse=True)
            x = x + (y * (h_fwd + h_bwd)) @ od_w_out[i]
            h = rms_norm(x, od_ffn_norm[i])
            x = x + moe_swiglu(h, od_router[i], od_moe_w_gate[i], od_moe_w_up[i], od_moe_w_down[i])
    return rms_norm(x, final_norm)
```

```python
import functools
import math

import numpy as np
import jax
import jax.numpy as jnp
from jax import lax
from jax.experimental import pallas as pl
from jax.experimental.pallas import tpu as pltpu

F32 = jnp.float32
BF16 = jnp.bfloat16

HEAD_DIM = 128
NA_HEADS = 8
DA_HEADS = 8
GRID_W = 64
NA_KH = 8
NA_KW = 16
ROPE_THETA = 10000.0
LRU_BLOCK_DIM = 128
LRU_C = 8.0
N_EXPERTS = 8
EPS = 1e-6
NEG_INF = -1e30

NA_Q_ROWS = 8
NA_K_ROWS = NA_Q_ROWS + NA_KH
DA_Q_BLOCK = 256
MOE_ROW_TILE = 256
ROUTE_TILE = 256
LANES = 128
SUBLANES = 8
VMEM_LIMIT = 56 * 1024 * 1024


def _cparams(semantics):
    return pltpu.CompilerParams(dimension_semantics=semantics, vmem_limit_bytes=VMEM_LIMIT)


def _rms(x, g):
    return x * lax.rsqrt(jnp.mean(x * x, axis=-1, keepdims=True) + EPS) * g


def _rmsnorm_kernel(x_ref, g_ref, o_ref):
    o_ref[...] = _rms(x_ref[...], g_ref[...]).astype(o_ref.dtype)


def rmsnorm(x, g, out_dtype, tm=512):
    t, d = x.shape
    return pl.pallas_call(
        _rmsnorm_kernel,
        grid=(t // tm,),
        in_specs=[pl.BlockSpec((tm, d), lambda i: (i, 0)),
                  pl.BlockSpec((1, d), lambda i: (0, 0))],
        out_specs=pl.BlockSpec((tm, d), lambda i: (i, 0)),
        out_shape=jax.ShapeDtypeStruct((t, d), out_dtype),
        compiler_params=_cparams(("arbitrary",)),
        name="rmsnorm",
    )(x, g.reshape(1, d))


CAST_ROWS = 256


def _cast_weight(w_ref, wbf_ref):
    k = w_ref.shape[0]

    def body(c, carry):
        r = pl.multiple_of(c * CAST_ROWS, CAST_ROWS)
        wbf_ref[pl.ds(r, CAST_ROWS), :] = w_ref[pl.ds(r, CAST_ROWS), :].astype(BF16)
        return carry

    lax.fori_loop(0, k // CAST_ROWS, body, 0)


def _mm_kernel(te_ref, tf_ref, tv_ref, a_ref, *refs, n_w, epilogue):
    del te_ref
    w_refs = refs[:n_w]
    pos = n_w
    res_ref = None
    if epilogue == "residual":
        res_ref = refs[pos]
        pos += 1
    o_ref = refs[pos]
    wbf_refs = refs[pos + 1:pos + 1 + n_w]
    i = pl.program_id(1)

    @pl.when(tf_ref[i] == 1)
    def _():
        for w_ref, wbf_ref in zip(w_refs, wbf_refs):
            _cast_weight(w_ref, wbf_ref)

    @pl.when(tv_ref[i] == 1)
    def _():
        a = a_ref[...]
        acc = jnp.dot(a, wbf_refs[0][...], preferred_element_type=F32)
        if epilogue == "swiglu":
            up = jnp.dot(a, wbf_refs[1][...], preferred_element_type=F32)
            acc = acc * jax.nn.sigmoid(acc) * up
        elif epilogue == "residual":
            acc = acc + res_ref[...]
        o_ref[...] = acc.astype(o_ref.dtype)

    @pl.when(tv_ref[i] == 0)
    def _():
        o_ref[...] = jnp.zeros_like(o_ref)


def ws_matmul(a, ws, *, tm, tn, epilogue, out_dtype, tiles=None, res=None, name):
    m, k = a.shape
    n = ws[0].shape[-1]
    n_i = m // tm
    if tiles is None:
        tiles = (jnp.zeros((n_i,), jnp.int32),
                 jnp.zeros((n_i,), jnp.int32).at[0].set(1),
                 jnp.ones((n_i,), jnp.int32))
    n_w = len(ws)
    in_specs = [pl.BlockSpec((tm, k), lambda j, i, te, tf, tv: (i, 0))]
    in_specs += [pl.BlockSpec((None, k, tn), lambda j, i, te, tf, tv: (te[i], 0, j))
                 for _ in ws]
    operands = [a, *ws]
    if epilogue == "residual":
        in_specs.append(pl.BlockSpec((tm, tn), lambda j, i, te, tf, tv: (i, j)))
        operands.append(res)
    grid_spec = pltpu.PrefetchScalarGridSpec(
        num_scalar_prefetch=3,
        grid=(n // tn, n_i),
        in_specs=in_specs,
        out_specs=pl.BlockSpec((tm, tn), lambda j, i, te, tf, tv: (i, j)),
        scratch_shapes=[pltpu.VMEM((k, tn), BF16) for _ in ws],
    )
    return pl.pallas_call(
        functools.partial(_mm_kernel, n_w=n_w, epilogue=epilogue),
        grid_spec=grid_spec,
        out_shape=jax.ShapeDtypeStruct((m, n), out_dtype),
        compiler_params=_cparams(("arbitrary", "arbitrary")),
        name=name,
    )(*tiles, *operands)


def _na_bias_tiles(rpb, rows):
    n_heads = rpb.shape[0]
    qr = np.arange(NA_Q_ROWS)
    kr = np.arange(NA_K_ROWS)
    c = np.arange(GRID_W)
    c_start = np.clip(c - NA_KW // 2, 0, GRID_W - NA_KW)
    col_in = (c[None, :] >= c_start[:, None]) & (c[None, :] < c_start[:, None] + NA_KW)
    col_off = np.clip(c[None, :] - c[:, None], -(NA_KW - 1), NA_KW - 1)
    tiles = []
    for r0, ks in ((0, 0), (NA_Q_ROWS, NA_Q_ROWS - NA_KH // 2), (rows - NA_Q_ROWS, rows - NA_K_ROWS)):
        qrow = r0 + qr
        krow = ks + kr
        r_start = np.clip(qrow - NA_KH // 2, 0, rows - NA_KH)
        row_in = (krow[None, :] >= r_start[:, None]) & (krow[None, :] < r_start[:, None] + NA_KH)
        row_off = np.clip(krow[None, :] - qrow[:, None], -(NA_KH - 1), NA_KH - 1)
        bias = rpb[:, (row_off + NA_KH - 1)[:, None, :, None],
                   (col_off + NA_KW - 1)[None, :, None, :]].astype(F32)
        mask = row_in[:, None, :, None] & col_in[None, :, None, :]
        tile = jnp.where(mask[None], bias, NEG_INF)
        tiles.append(tile.reshape(n_heads, NA_Q_ROWS * GRID_W, NA_K_ROWS * GRID_W))
    return jnp.stack(tiles, axis=1)


def _na_kernel(q_ref, k_ref, v_ref, b_ref, o_ref, *, rows):
    rb = pl.program_id(2)
    ks = jnp.clip(rb * NA_Q_ROWS - NA_KH // 2, 0, rows - NA_K_ROWS)
    k0 = pl.multiple_of(ks * GRID_W, GRID_W)
    nk = NA_K_ROWS * GRID_W
    kb = k_ref[pl.ds(k0, nk), :]
    vb = v_ref[pl.ds(k0, nk), :]
    s = lax.dot_general(q_ref[...], kb, (((1,), (1,)), ((), ())), preferred_element_type=F32)
    s = s * (HEAD_DIM ** -0.5) + b_ref[...]
    m = jnp.max(s, axis=-1, keepdims=True)
    p = jnp.exp(s - m)
    l = jnp.sum(p, axis=-1, keepdims=True)
    o = jnp.dot(p.astype(BF16), vb, preferred_element_type=F32)
    o_ref[...] = (o / l).astype(o_ref.dtype)


def neighbourhood_attention(proj, rpb):
    b, s, _ = proj.shape
    rows = s // GRID_W
    n_rb = rows // NA_Q_ROWS
    tq = NA_Q_ROWS * GRID_W
    bias = _na_bias_tiles(rpb, rows)

    def pat(r):
        return jnp.where(r == 0, 0, jnp.where(r == n_rb - 1, 2, 1))

    return pl.pallas_call(
        functools.partial(_na_kernel, rows=rows),
        grid=(b, NA_HEADS, n_rb),
        in_specs=[
            pl.BlockSpec((None, tq, HEAD_DIM), lambda bi, h, r: (bi, r, h)),
            pl.BlockSpec((None, s, HEAD_DIM), lambda bi, h, r: (bi, 0, NA_HEADS + h)),
            pl.BlockSpec((None, s, HEAD_DIM), lambda bi, h, r: (bi, 0, 2 * NA_HEADS + h)),
            pl.BlockSpec((None, None, tq, NA_K_ROWS * GRID_W), lambda bi, h, r: (h, pat(r), 0, 0)),
        ],
        out_specs=pl.BlockSpec((None, tq, HEAD_DIM), lambda bi, h, r: (bi, r, h)),
        out_shape=jax.ShapeDtypeStruct((b, s, NA_HEADS * HEAD_DIM), BF16),
        compiler_params=_cparams(("arbitrary", "arbitrary", "arbitrary")),
        name="na_attention",
    )(proj, proj, proj, bias)


def _rope_tables(seq):
    dq = HEAD_DIM // 2
    inv = 1.0 / (ROPE_THETA ** (jnp.arange(0, dq, 2, dtype=F32) / dq))
    ang = jnp.arange(seq, dtype=F32)[:, None] * inv[None, :]
    cos, sin = jnp.cos(ang), jnp.sin(ang)
    zero = jnp.zeros_like(sin)
    c = jnp.concatenate([cos, cos, cos, cos], axis=-1)
    s_first = jnp.concatenate([-sin, zero, -sin, zero], axis=-1)
    s_second = jnp.concatenate([zero, sin, zero, sin], axis=-1)
    return c, s_first, s_second


def _rope(x, c, s_first, s_second):
    half = HEAD_DIM // 4
    return (x * c + pltpu.roll(x, HEAD_DIM - half, axis=1) * s_first
            + pltpu.roll(x, half, axis=1) * s_second)


def _da_kernel(lam_ref, q_ref, k_ref, v_ref, cq_ref, s1q_ref, s2q_ref, ck_ref, s1k_ref, s2k_ref,
               g_ref, o_ref, kr_ref, *, out_scale):
    @pl.when(pl.program_id(2) == 0)
    def _():
        k = k_ref[...].astype(F32)
        kr_ref[...] = _rope(k, ck_ref[...], s1k_ref[...], s2k_ref[...]).astype(BF16)

    dq = HEAD_DIM // 2
    q = _rope(q_ref[...].astype(F32), cq_ref[...], s1q_ref[...], s2q_ref[...]) * (dq ** -0.5)
    lane = lax.broadcasted_iota(jnp.int32, q.shape, 1)
    q1 = jnp.where(lane < dq, q, 0.0).astype(BF16)
    q2 = jnp.where(lane >= dq, q, 0.0).astype(BF16)
    kr = kr_ref[...]
    dn = (((1,), (1,)), ((), ()))
    s1 = lax.dot_general(q1, kr, dn, preferred_element_type=F32)
    s2 = lax.dot_general(q2, kr, dn, preferred_element_type=F32)
    p1 = jnp.exp(s1 - jnp.max(s1, axis=-1, keepdims=True))
    p2 = jnp.exp(s2 - jnp.max(s2, axis=-1, keepdims=True))
    w1 = 1.0 / jnp.sum(p1, axis=-1, keepdims=True)
    w2 = lam_ref[0] / jnp.sum(p2, axis=-1, keepdims=True)
    attn = (p1 * w1 - p2 * w2).astype(BF16)
    o = jnp.dot(attn, v_ref[...], preferred_element_type=F32)
    o_ref[...] = (_rms(o, g_ref[...]) * out_scale).astype(o_ref.dtype)


def differential_attention(proj, lam, subln, lambda_init):
    b, s, _ = proj.shape
    tq = min(DA_Q_BLOCK, s)
    base = 3 * NA_HEADS
    c, s_first, s_second = _rope_tables(s)
    qtab = pl.BlockSpec((tq, HEAD_DIM), lambda bi, h, i: (i, 0))
    ktab = pl.BlockSpec((s, HEAD_DIM), lambda bi, h, i: (0, 0))
    return pl.pallas_call(
        functools.partial(_da_kernel, out_scale=1.0 - lambda_init),
        grid=(b, DA_HEADS, s // tq),
        in_specs=[
            pl.BlockSpec(memory_space=pltpu.SMEM),
            pl.BlockSpec((None, tq, HEAD_DIM), lambda bi, h, i: (bi, i, base + h)),
            pl.BlockSpec((None, s, HEAD_DIM), lambda bi, h, i: (bi, 0, base + DA_HEADS + h)),
            pl.BlockSpec((None, s, HEAD_DIM), lambda bi, h, i: (bi, 0, base + 2 * DA_HEADS + h)),
            qtab, qtab, qtab, ktab, ktab, ktab,
            pl.BlockSpec((1, HEAD_DIM), lambda bi, h, i: (0, 0)),
        ],
        out_specs=pl.BlockSpec((None, tq, HEAD_DIM), lambda bi, h, i: (bi, i, h)),
        out_shape=jax.ShapeDtypeStruct((b, s, DA_HEADS * HEAD_DIM), BF16),
        scratch_shapes=[pltpu.VMEM((s, HEAD_DIM), BF16)],
        compiler_params=_cparams(("arbitrary", "arbitrary", "arbitrary")),
        name="da_attention",
    )(lam, proj, proj, proj, c, s_first, s_second, c, s_first, s_second, subln.reshape(1, HEAD_DIM))


SCAN_CHUNK = 512


def _shift_rows(x, k):
    return pltpu.roll(x, k % x.shape[0], axis=0)


def _block_scan(a, b, reverse):
    row = lax.broadcasted_iota(jnp.int32, a.shape, 0) % SUBLANES
    k = 1
    while k < SUBLANES:
        if reverse:
            keep = row < SUBLANES - k
            a_n, b_n = _shift_rows(a, -k), _shift_rows(b, -k)
        else:
            keep = row >= k
            a_n, b_n = _shift_rows(a, k), _shift_rows(b, k)
        b = jnp.where(keep, a * b_n + b, b)
        a = jnp.where(keep, a * a_n, a)
        k *= 2
    return a, b


def _lru_kernel(gate_ref, rec_ref, cw_ref, cb_ref, wa_ref, ba_ref, wx_ref, bx_ref, ap_ref, o_ref,
                u_ref, a_ref, b_ref, h_ref):
    s = rec_ref.shape[0]
    n_chunks = s // SCAN_CHUNK

    r = rec_ref[...].astype(F32)
    t_idx = lax.broadcasted_iota(jnp.int32, r.shape, 0)
    cw = cw_ref[...]
    u = r * cw[1:2, :] + cb_ref[...]
    u = u + jnp.where(t_idx >= 1, _shift_rows(r, 1), 0.0) * cw[0:1, :]
    u = u + jnp.where(t_idx < s - 1, _shift_rows(r, -1), 0.0) * cw[2:3, :]
    u = u + jnp.where(t_idx < s - 2, _shift_rows(r, -2), 0.0) * cw[3:4, :]
    u_ref[...] = u

    for d in range(2):
        reverse = d == 1
        ap = ap_ref[d]
        softplus = jnp.maximum(-ap, 0.0) + jnp.log1p(jnp.exp(-jnp.abs(ap)))
        wa = wa_ref[d].astype(BF16)
        wx = wx_ref[d].astype(BF16)

        def prep(c, carry, d=d, reverse=reverse, softplus=softplus, wa=wa, wx=wx):
            r0 = pl.multiple_of(c * SCAN_CHUNK, SCAN_CHUNK)
            uc = u_ref[pl.ds(r0, SCAN_CHUNK), :]
            ub = uc.astype(BF16)
            ga = jax.nn.sigmoid(jnp.dot(ub, wa, preferred_element_type=F32) + ba_ref[d])
            gx = jax.nn.sigmoid(jnp.dot(ub, wx, preferred_element_type=F32) + bx_ref[d])
            log_a = -LRU_C * ga * softplus
            a = jnp.exp(log_a)
            mult = jnp.sqrt(1.0 - jnp.exp(2.0 * log_a))
            a_s, b_s = _block_scan(a, mult * (gx * uc), reverse)
            a_ref[pl.ds(r0, SCAN_CHUNK), :] = a_s
            b_ref[pl.ds(r0, SCAN_CHUNK), :] = b_s
            return carry

        lax.fori_loop(0, n_chunks, prep, 0)

        n_groups = s // SUBLANES

        def step(i, h_prev, reverse=reverse):
            g = (n_groups - 1 - i) if reverse else i
            r0 = pl.multiple_of(g * SUBLANES, SUBLANES)
            h = a_ref[pl.ds(r0, SUBLANES), :] * h_prev + b_ref[pl.ds(r0, SUBLANES), :]
            if reverse:
                h_ref[pl.ds(r0, SUBLANES), :] = h_ref[pl.ds(r0, SUBLANES), :] + h
            else:
                h_ref[pl.ds(r0, SUBLANES), :] = h
            edge = h[0:1, :] if reverse else h[SUBLANES - 1:SUBLANES, :]
            return jnp.broadcast_to(edge, h.shape)

        lax.fori_loop(0, n_groups, step, jnp.zeros((SUBLANES, LRU_BLOCK_DIM), F32), unroll=8)

    y = jax.nn.gelu(gate_ref[...].astype(F32), approximate=True)
    o_ref[...] = (y * h_ref[...]).astype(o_ref.dtype)


def recurrent_block(proj, conv_w, conv_b, w_a, b_a, w_x, b_x, a_param):
    b, s, w2 = proj.shape
    w = w2 // 2
    nb = w // LRU_BLOCK_DIM
    c = LRU_BLOCK_DIM
    vec = lambda p: p.reshape(2, 1, w)
    return pl.pallas_call(
        _lru_kernel,
        grid=(b, nb),
        in_specs=[
            pl.BlockSpec((None, s, c), lambda bi, n: (bi, 0, n)),
            pl.BlockSpec((None, s, c), lambda bi, n: (bi, 0, nb + n)),
            pl.BlockSpec((conv_w.shape[0], c), lambda bi, n: (0, n)),
            pl.BlockSpec((1, c), lambda bi, n: (0, n)),
            pl.BlockSpec((2, None, c, c), lambda bi, n: (0, n, 0, 0)),
            pl.BlockSpec((2, 1, c), lambda bi, n: (0, 0, n)),
            pl.BlockSpec((2, None, c, c), lambda bi, n: (0, n, 0, 0)),
            pl.BlockSpec((2, 1, c), lambda bi, n: (0, 0, n)),
            pl.BlockSpec((2, 1, c), lambda bi, n: (0, 0, n)),
        ],
        out_specs=pl.BlockSpec((None, s, c), lambda bi, n: (bi, 0, n)),
        out_shape=jax.ShapeDtypeStruct((b, s, w), BF16),
        scratch_shapes=[pltpu.VMEM((s, c), F32) for _ in range(4)],
        compiler_params=_cparams(("arbitrary", "arbitrary")),
        name="rg_lru",
    )(proj, proj, conv_w, conv_b.reshape(1, w), w_a, vec(b_a), w_x, vec(b_x), vec(a_param))


R_E0, R_E1, R_G0, R_G1, R_RANK0, R_RANK1 = range(6)


def _route_kernel(x_ref, g_ref, r_ref, meta_ref, cnt_ref, carry_ref):
    i = pl.program_id(0)

    @pl.when(i == 0)
    def _():
        carry_ref[...] = jnp.zeros_like(carry_ref)

    h = _rms(x_ref[...], g_ref[...])
    logits = jnp.dot(h, r_ref[...], preferred_element_type=F32, precision=lax.Precision.HIGHEST)
    tm = logits.shape[0]
    lane = lax.broadcasted_iota(jnp.int32, logits.shape, 1).astype(F32)
    logits = jnp.where(lane < N_EXPERTS, logits, -jnp.inf)
    m0 = jnp.max(logits, axis=-1, keepdims=True)
    e0 = jnp.min(jnp.where(logits == m0, lane, float(LANES)), axis=-1, keepdims=True)
    rest = jnp.where(lane == e0, -jnp.inf, logits)
    m1 = jnp.max(rest, axis=-1, keepdims=True)
    e1 = jnp.min(jnp.where(rest == m1, lane, float(LANES)), axis=-1, keepdims=True)
    z = jnp.exp(m1 - m0)
    g0 = 1.0 / (1.0 + z)
    g1 = z / (1.0 + z)

    hit0 = lane == e0
    hit1 = lane == e1
    both = jnp.where(hit0 | hit1, 1.0, 0.0).astype(BF16)
    ri = lax.broadcasted_iota(jnp.int32, (tm, tm), 0)
    ci = lax.broadcasted_iota(jnp.int32, (tm, tm), 1)
    lower = jnp.where(ci < ri, 1.0, 0.0).astype(BF16)
    before = jnp.dot(lower, both, preferred_element_type=F32) + carry_ref[0:1, :]
    rank0 = jnp.sum(jnp.where(hit0, before, 0.0), axis=-1, keepdims=True)
    rank1 = jnp.sum(jnp.where(hit1, before, 0.0), axis=-1, keepdims=True)
    total = carry_ref[0:1, :] + jnp.sum(both.astype(F32), axis=0, keepdims=True)
    carry_ref[...] = jnp.broadcast_to(total, carry_ref.shape)
    cnt_ref[...] = jnp.broadcast_to(total, cnt_ref.shape)

    rec = jnp.zeros(logits.shape, F32)
    for col, val in ((R_E0, e0), (R_E1, e1), (R_G0, g0), (R_G1, g1),
                     (R_RANK0, rank0), (R_RANK1, rank1)):
        rec = jnp.where(lane == col, val, rec)
    meta_ref[...] = rec


def route(x, g, router):
    t, d = x.shape
    tm = ROUTE_TILE
    r_pad = jnp.zeros((d, LANES), F32).at[:, :N_EXPERTS].set(router)
    return pl.pallas_call(
        _route_kernel,
        grid=(t // tm,),
        in_specs=[pl.BlockSpec((tm, d), lambda i: (i, 0)),
                  pl.BlockSpec((1, d), lambda i: (0, 0)),
                  pl.BlockSpec((d, LANES), lambda i: (0, 0))],
        out_specs=[pl.BlockSpec((tm, LANES), lambda i: (i, 0)),
                   pl.BlockSpec((SUBLANES, LANES), lambda i: (0, 0))],
        out_shape=[jax.ShapeDtypeStruct((t, LANES), F32),
                   jax.ShapeDtypeStruct((SUBLANES, LANES), F32)],
        scratch_shapes=[pltpu.VMEM((SUBLANES, LANES), F32)],
        compiler_params=_cparams(("arbitrary",)),
        name="moe_route",
    )(x, g.reshape(1, d), r_pad)


def _row_copy(src_ref, src_row, dst_ref, dst_row, sem):
    return pltpu.make_async_copy(src_ref.at[pl.ds(src_row, 1), :], dst_ref.at[pl.ds(dst_row, 1), :], sem)


def _dispatch_kernel(p0_ref, p1_ref, x_ref, g_ref, init_ref, xs_ref, h_ref, sem):
    del init_ref
    tm = x_ref.shape[0]
    base = pl.program_id(0) * tm
    h_ref[...] = _rms(x_ref[...], g_ref[...])

    def start(t, carry):
        _row_copy(h_ref, t, xs_ref, p0_ref[base + t], sem).start()
        _row_copy(h_ref, t, xs_ref, p1_ref[base + t], sem).start()
        return carry

    lax.fori_loop(0, tm, start, 0)

    def wait(t, carry):
        _row_copy(h_ref, 0, xs_ref, 0, sem).wait()
        _row_copy(h_ref, 0, xs_ref, 0, sem).wait()
        return carry

    lax.fori_loop(0, tm, wait, 0)


def dispatch(x, g, pos0, pos1, n_rows):
    t, d = x.shape
    tm = ROUTE_TILE
    grid_spec = pltpu.PrefetchScalarGridSpec(
        num_scalar_prefetch=2,
        grid=(t // tm,),
        in_specs=[pl.BlockSpec((tm, d), lambda i, p0, p1: (i, 0)),
                  pl.BlockSpec((1, d), lambda i, p0, p1: (0, 0)),
                  pl.BlockSpec(memory_space=pl.ANY)],
        out_specs=pl.BlockSpec(memory_space=pl.ANY),
        scratch_shapes=[pltpu.VMEM((tm, d), F32), pltpu.SemaphoreType.DMA(())],
    )
    return pl.pallas_call(
        _dispatch_kernel,
        grid_spec=grid_spec,
        out_shape=jax.ShapeDtypeStruct((n_rows, d), F32),
        input_output_aliases={4: 0},
        compiler_params=_cparams(("arbitrary",)),
        name="moe_dispatch",
    )(pos0, pos1, x, g.reshape(1, d), jnp.zeros((n_rows, d), F32))


def _combine_kernel(p0_ref, p1_ref, x_ref, meta_ref, g_ref, y_ref, o_ref, y0_ref, y1_ref, sem):
    tm = x_ref.shape[0]
    base = pl.program_id(0) * tm

    def start(t, carry):
        _row_copy(y_ref, p0_ref[base + t], y0_ref, t, sem).start()
        _row_copy(y_ref, p1_ref[base + t], y1_ref, t, sem).start()
        return carry

    lax.fori_loop(0, tm, start, 0)

    def wait(t, carry):
        _row_copy(y_ref, 0, y0_ref, 0, sem).wait()
        _row_copy(y_ref, 0, y1_ref, 0, sem).wait()
        return carry

    lax.fori_loop(0, tm, wait, 0)
    meta = meta_ref[...]
    g0 = meta[:, R_G0:R_G0 + 1]
    g1 = meta[:, R_G1:R_G1 + 1]
    out = x_ref[...] + g0 * y0_ref[...] + g1 * y1_ref[...]
    o_ref[...] = _rms(out, g_ref[...])


def combine(x, meta, g, y, pos0, pos1):
    t, d = x.shape
    tm = ROUTE_TILE
    grid_spec = pltpu.PrefetchScalarGridSpec(
        num_scalar_prefetch=2,
        grid=(t // tm,),
        in_specs=[pl.BlockSpec((tm, d), lambda i, p0, p1: (i, 0)),
                  pl.BlockSpec((tm, LANES), lambda i, p0, p1: (i, 0)),
                  pl.BlockSpec((1, d), lambda i, p0, p1: (0, 0)),
                  pl.BlockSpec(memory_space=pl.ANY)],
        out_specs=pl.BlockSpec((tm, d), lambda i, p0, p1: (i, 0)),
        scratch_shapes=[pltpu.VMEM((tm, d), F32), pltpu.VMEM((tm, d), F32),
                        pltpu.SemaphoreType.DMA(())],
    )
    return pl.pallas_call(
        _combine_kernel,
        grid_spec=grid_spec,
        out_shape=jax.ShapeDtypeStruct((t, d), F32),
        compiler_params=_cparams(("arbitrary",)),
        name="moe_combine",
    )(pos0, pos1, x, meta, g.reshape(1, d), y)


def _moe_plan(meta, counts, n_tiles, tm):
    counts = counts[0, :N_EXPERTS].astype(jnp.int32)
    padded = ((counts + tm - 1) // tm) * tm
    ends = jnp.cumsum(padded)
    starts = ends - padded
    e0 = meta[:, R_E0].astype(jnp.int32)
    e1 = meta[:, R_E1].astype(jnp.int32)
    pos0 = starts[e0] + meta[:, R_RANK0].astype(jnp.int32)
    pos1 = starts[e1] + meta[:, R_RANK1].astype(jnp.int32)
    n_used = ends[-1] // tm
    tile_idx = jnp.arange(n_tiles, dtype=jnp.int32)
    last = jnp.maximum(n_used - 1, 0)
    owner = jnp.sum((jnp.minimum(tile_idx, last)[:, None] * tm >= ends[None, :]).astype(jnp.int32), axis=1)
    owner = jnp.minimum(owner, N_EXPERTS - 1)
    first = jnp.concatenate([jnp.ones((1,), jnp.int32),
                             (owner[1:] != owner[:-1]).astype(jnp.int32)])
    valid = (tile_idx < n_used).astype(jnp.int32)
    return pos0, pos1, (owner, first, valid)


def kernel(x, ev_mix_norm, ev_w_in, ev_na_rpb, ev_da_lambda_q1, ev_da_lambda_k1, ev_da_lambda_q2, ev_da_lambda_k2, ev_da_subln, ev_w_out, ev_ffn_norm, ev_ffn_w_gate, ev_ffn_w_up, ev_ffn_w_down, od_mix_norm, od_w_in, od_conv_w, od_conv_b, od_lru_w_a, od_lru_b_a, od_lru_w_x, od_lru_b_x, od_lru_a_param, od_w_out, od_ffn_norm, od_router, od_moe_w_gate, od_moe_w_up, od_moe_w_down, final_norm):
    b, s, d = x.shape
    t = b * s
    xt = x.reshape(t, d)

    lambda_init = 0.8 - 0.6 * math.exp(-0.3 * 0)
    h = rmsnorm(xt, ev_mix_norm[0], BF16)
    proj = ws_matmul(h, [ev_w_in], tm=512, tn=1024, epilogue="cast", out_dtype=BF16, name="l0_in_proj")
    proj = proj.reshape(b, s, -1)
    na_out = neighbourhood_attention(proj, ev_na_rpb[0])
    lam = (jnp.exp(jnp.sum(ev_da_lambda_q1[0] * ev_da_lambda_k1[0]))
           - jnp.exp(jnp.sum(ev_da_lambda_q2[0] * ev_da_lambda_k2[0])) + lambda_init).reshape(1)
    da_out = differential_attention(proj, lam, ev_da_subln[0], lambda_init)
    mix = jnp.concatenate([na_out, da_out], axis=-1).reshape(t, -1)
    xt = ws_matmul(mix, [ev_w_out], tm=512, tn=1024, epilogue="residual", out_dtype=F32, res=xt,
                   name="l0_out_proj")

    h = rmsnorm(xt, ev_ffn_norm[0], BF16)
    act = ws_matmul(h, [ev_ffn_w_gate, ev_ffn_w_up], tm=512, tn=512, epilogue="swiglu",
                    out_dtype=BF16, name="l0_ffn_up")
    xt = ws_matmul(act, [ev_ffn_w_down], tm=256, tn=512, epilogue="residual", out_dtype=F32, res=xt,
                   name="l0_ffn_down")

    h = rmsnorm(xt, od_mix_norm[0], BF16)
    proj = ws_matmul(h, [od_w_in], tm=512, tn=1024, epilogue="cast", out_dtype=BF16, name="l1_in_proj")
    rec = recurrent_block(proj.reshape(b, s, -1), od_conv_w[0], od_conv_b[0], od_lru_w_a[0],
                          od_lru_b_a[0], od_lru_w_x[0], od_lru_b_x[0], od_lru_a_param[0])
    xt = ws_matmul(rec.reshape(t, -1), [od_w_out], tm=512, tn=1024, epilogue="residual",
                   out_dtype=F32, res=xt, name="l1_out_proj")

    tm = MOE_ROW_TILE
    n_tiles = (t * 2) // tm + N_EXPERTS
    meta, counts = route(xt, od_ffn_norm[0], od_router[0])
    pos0, pos1, tiles = _moe_plan(meta, counts, n_tiles, tm)
    xs = dispatch(xt, od_ffn_norm[0], pos0, pos1, n_tiles * tm)
    act = ws_matmul(xs.astype(BF16), [od_moe_w_gate[0], od_moe_w_up[0]], tm=tm, tn=512,
                    epilogue="swiglu", out_dtype=BF16, tiles=tiles, name="moe_up")
    y = ws_matmul(act, [od_moe_w_down[0]], tm=tm, tn=512, epilogue="cast", out_dtype=F32,
                  tiles=tiles, name="moe_down")
    out = combine(xt, meta, final_norm, y, pos0, pos1)
    return out.reshape(b, s, d)
```

```python
import functools
import math

import numpy as np
import jax
import jax.numpy as jnp
from jax import lax
from jax.experimental import pallas as pl
from jax.experimental.pallas import tpu as pltpu

F32 = jnp.float32
BF16 = jnp.bfloat16

HEAD_DIM = 128
NA_HEADS = 8
DA_HEADS = 8
GRID_W = 64
NA_KH = 8
NA_KW = 16
ROPE_THETA = 10000.0
LRU_BLOCK_DIM = 128
LRU_C = 8.0
N_EXPERTS = 8
EPS = 1e-6
NEG_INF = -1e30
LOG2E = math.log2(math.e)

NA_Q_ROWS = 8
NA_K_ROWS = NA_Q_ROWS + NA_KH
DA_Q_BLOCK = 256
MOE_ROW_TILE = 256
ROUTE_TILE = 256
LANES = 128
SUBLANES = 8
VMEM_LIMIT = 56 * 1024 * 1024


def _cparams(semantics):
    return pltpu.CompilerParams(dimension_semantics=semantics, vmem_limit_bytes=VMEM_LIMIT)


def _rms(x, g):
    return x * lax.rsqrt(jnp.mean(x * x, axis=-1, keepdims=True) + EPS) * g


def _rmsnorm_kernel(x_ref, g_ref, o_ref):
    o_ref[...] = _rms(x_ref[...], g_ref[...]).astype(o_ref.dtype)


def rmsnorm(x, g, out_dtype, tm=512):
    t, d = x.shape
    return pl.pallas_call(
        _rmsnorm_kernel,
        grid=(t // tm,),
        in_specs=[pl.BlockSpec((tm, d), lambda i: (i, 0)),
                  pl.BlockSpec((1, d), lambda i: (0, 0))],
        out_specs=pl.BlockSpec((tm, d), lambda i: (i, 0)),
        out_shape=jax.ShapeDtypeStruct((t, d), out_dtype),
        compiler_params=_cparams(("arbitrary",)),
        name="rmsnorm",
    )(x, g.reshape(1, d))


CAST_ROWS = 256


def _cast_weight(w_ref, wbf_ref):
    k = w_ref.shape[0]

    def body(c, carry):
        r = pl.multiple_of(c * CAST_ROWS, CAST_ROWS)
        wbf_ref[pl.ds(r, CAST_ROWS), :] = w_ref[pl.ds(r, CAST_ROWS), :].astype(BF16)
        return carry

    lax.fori_loop(0, k // CAST_ROWS, body, 0)


def _mm_kernel(te_ref, tf_ref, tv_ref, a_ref, *refs, n_w, epilogue):
    del te_ref
    w_refs = refs[:n_w]
    pos = n_w
    res_ref = None
    if epilogue == "residual":
        res_ref = refs[pos]
        pos += 1
    o_ref = refs[pos]
    wbf_refs = refs[pos + 1:pos + 1 + n_w]
    i = pl.program_id(1)

    @pl.when(tf_ref[i] == 1)
    def _():
        for w_ref, wbf_ref in zip(w_refs, wbf_refs):
            _cast_weight(w_ref, wbf_ref)

    @pl.when(tv_ref[i] == 1)
    def _():
        a = a_ref[...].astype(BF16)
        acc = jnp.dot(a, wbf_refs[0][...], preferred_element_type=F32)
        if epilogue == "swiglu":
            up = jnp.dot(a, wbf_refs[1][...], preferred_element_type=F32)
            acc = acc * jax.nn.sigmoid(acc) * up
        elif epilogue == "residual":
            acc = acc + res_ref[...]
        o_ref[...] = acc.astype(o_ref.dtype)

    @pl.when(tv_ref[i] == 0)
    def _():
        o_ref[...] = jnp.zeros_like(o_ref)


def ws_matmul(a, ws, *, tm, tn, epilogue, out_dtype, tiles=None, res=None, name):
    m, k = a.shape
    n = ws[0].shape[-1]
    n_i = m // tm
    if tiles is None:
        tiles = (jnp.zeros((n_i,), jnp.int32),
                 jnp.zeros((n_i,), jnp.int32).at[0].set(1),
                 jnp.ones((n_i,), jnp.int32))
    n_w = len(ws)
    in_specs = [pl.BlockSpec((tm, k), lambda j, i, te, tf, tv: (i, 0))]
    in_specs += [pl.BlockSpec((None, k, tn), lambda j, i, te, tf, tv: (te[i], 0, j))
                 for _ in ws]
    operands = [a, *ws]
    if epilogue == "residual":
        in_specs.append(pl.BlockSpec((tm, tn), lambda j, i, te, tf, tv: (i, j)))
        operands.append(res)
    grid_spec = pltpu.PrefetchScalarGridSpec(
        num_scalar_prefetch=3,
        grid=(n // tn, n_i),
        in_specs=in_specs,
        out_specs=pl.BlockSpec((tm, tn), lambda j, i, te, tf, tv: (i, j)),
        scratch_shapes=[pltpu.VMEM((k, tn), BF16) for _ in ws],
    )
    return pl.pallas_call(
        functools.partial(_mm_kernel, n_w=n_w, epilogue=epilogue),
        grid_spec=grid_spec,
        out_shape=jax.ShapeDtypeStruct((m, n), out_dtype),
        compiler_params=_cparams(("arbitrary", "arbitrary")),
        name=name,
    )(*tiles, *operands)


def _na_bias_tiles(rpb, rows):
    n_heads, _, n_co = rpb.shape
    c = np.arange(GRID_W)
    c_start = np.clip(c - NA_KW // 2, 0, GRID_W - NA_KW)
    col_in = (c[None, :] >= c_start[:, None]) & (c[None, :] < c_start[:, None] + NA_KW)
    col_off = np.clip(c[None, :] - c[:, None], -(NA_KW - 1), NA_KW - 1) + NA_KW - 1
    pick = (col_off[None] == np.arange(n_co)[:, None, None]).astype(np.float32)
    blocks = jnp.einsum("hrj,jqk->hrqk", rpb.astype(F32) * LOG2E, pick,
                        precision=lax.Precision.HIGHEST)
    blocks = jnp.where(col_in[None, None], blocks, NEG_INF)
    masked = jnp.full((n_heads, GRID_W, GRID_W), NEG_INF, F32)
    tiles = []
    for r0, ks in ((0, 0), (NA_Q_ROWS, NA_Q_ROWS - NA_KH // 2), (rows - NA_Q_ROWS, rows - NA_K_ROWS)):
        q_rows = []
        for qrow in range(r0, r0 + NA_Q_ROWS):
            r_start = min(max(qrow - NA_KH // 2, 0), rows - NA_KH)
            q_rows.append(jnp.concatenate(
                [blocks[:, krow - qrow + NA_KH - 1] if r_start <= krow < r_start + NA_KH else masked
                 for krow in range(ks, ks + NA_K_ROWS)], axis=-1))
        tiles.append(jnp.concatenate(q_rows, axis=1))
    return jnp.stack(tiles, axis=1)


def _na_kernel(q_ref, k_ref, v_ref, b_ref, o_ref, *, rows):
    rb = pl.program_id(2)
    ks = jnp.clip(rb * NA_Q_ROWS - NA_KH // 2, 0, rows - NA_K_ROWS)
    k0 = pl.multiple_of(ks * GRID_W, GRID_W)
    nk = NA_K_ROWS * GRID_W
    kb = k_ref[pl.ds(k0, nk), :]
    vb = v_ref[pl.ds(k0, nk), :]
    s = lax.dot_general(q_ref[...], kb, (((1,), (1,)), ((), ())), preferred_element_type=F32)
    s = s * (HEAD_DIM ** -0.5 * LOG2E) + b_ref[...]
    p = jnp.exp2(s - jnp.max(s, axis=-1, keepdims=True)).astype(BF16)
    v1 = jnp.concatenate([vb, jnp.ones_like(vb)], axis=-1)
    o = jnp.dot(p, v1, preferred_element_type=F32)
    o_ref[...] = (o[:, :HEAD_DIM] / o[:, HEAD_DIM:]).astype(o_ref.dtype)


def neighbourhood_attention(proj, rpb):
    b, s, _ = proj.shape
    rows = s // GRID_W
    n_rb = rows // NA_Q_ROWS
    tq = NA_Q_ROWS * GRID_W
    bias = _na_bias_tiles(rpb, rows)

    def pat(r):
        return jnp.where(r == 0, 0, jnp.where(r == n_rb - 1, 2, 1))

    return pl.pallas_call(
        functools.partial(_na_kernel, rows=rows),
        grid=(b, NA_HEADS, n_rb),
        in_specs=[
            pl.BlockSpec((None, tq, HEAD_DIM), lambda bi, h, r: (bi, r, h)),
            pl.BlockSpec((None, s, HEAD_DIM), lambda bi, h, r: (bi, 0, NA_HEADS + h)),
            pl.BlockSpec((None, s, HEAD_DIM), lambda bi, h, r: (bi, 0, 2 * NA_HEADS + h)),
            pl.BlockSpec((None, None, tq, NA_K_ROWS * GRID_W), lambda bi, h, r: (h, pat(r), 0, 0)),
        ],
        out_specs=pl.BlockSpec((None, tq, HEAD_DIM), lambda bi, h, r: (bi, r, h)),
        out_shape=jax.ShapeDtypeStruct((b, s, NA_HEADS * HEAD_DIM), BF16),
        compiler_params=_cparams(("arbitrary", "arbitrary", "arbitrary")),
        name="na_attention",
    )(proj, proj, proj, bias)


def _rope_tables(seq):
    dq = HEAD_DIM // 2
    inv = 1.0 / (ROPE_THETA ** (jnp.arange(0, dq, 2, dtype=F32) / dq))
    ang = jnp.arange(seq, dtype=F32)[:, None] * inv[None, :]
    cos, sin = jnp.cos(ang), jnp.sin(ang)
    zero = jnp.zeros_like(sin)
    c = jnp.concatenate([cos, cos, cos, cos], axis=-1)
    s_first = jnp.concatenate([-sin, zero, -sin, zero], axis=-1)
    s_second = jnp.concatenate([zero, sin, zero, sin], axis=-1)
    return c, s_first, s_second


def _rope(x, c, s_first, s_second):
    half = HEAD_DIM // 4
    return (x * c + pltpu.roll(x, HEAD_DIM - half, axis=1) * s_first
            + pltpu.roll(x, half, axis=1) * s_second)


def _da_kernel(lam_ref, q_ref, k_ref, v_ref, cq_ref, s1q_ref, s2q_ref, ck_ref, s1k_ref, s2k_ref,
               g_ref, o_ref, kr_ref, *, out_scale):
    @pl.when(pl.program_id(2) == 0)
    def _():
        k = k_ref[...].astype(F32)
        kr_ref[...] = _rope(k, ck_ref[...], s1k_ref[...], s2k_ref[...]).astype(BF16)

    dq = HEAD_DIM // 2
    q = _rope(q_ref[...].astype(F32), cq_ref[...], s1q_ref[...], s2q_ref[...]) * (dq ** -0.5 * LOG2E)
    lane = lax.broadcasted_iota(jnp.int32, q.shape, 1)
    q1 = jnp.where(lane < dq, q, 0.0).astype(BF16)
    q2 = jnp.where(lane >= dq, q, 0.0).astype(BF16)
    kr = kr_ref[...]
    dn = (((1,), (1,)), ((), ()))
    s1 = lax.dot_general(q1, kr, dn, preferred_element_type=F32)
    s2 = lax.dot_general(q2, kr, dn, preferred_element_type=F32)
    p1 = jnp.exp2(s1 - jnp.max(s1, axis=-1, keepdims=True))
    p2 = jnp.exp2(s2 - jnp.max(s2, axis=-1, keepdims=True))
    w1 = 1.0 / jnp.sum(p1, axis=-1, keepdims=True)
    w2 = lam_ref[0] / jnp.sum(p2, axis=-1, keepdims=True)
    attn = (p1 * w1 - p2 * w2).astype(BF16)
    o = jnp.dot(attn, v_ref[...], preferred_element_type=F32)
    o_ref[...] = (_rms(o, g_ref[...]) * out_scale).astype(o_ref.dtype)


def differential_attention(proj, lam, subln, lambda_init):
    b, s, _ = proj.shape
    tq = min(DA_Q_BLOCK, s)
    base = 3 * NA_HEADS
    c, s_first, s_second = _rope_tables(s)
    qtab = pl.BlockSpec((tq, HEAD_DIM), lambda bi, h, i: (i, 0))
    ktab = pl.BlockSpec((s, HEAD_DIM), lambda bi, h, i: (0, 0))
    return pl.pallas_call(
        functools.partial(_da_kernel, out_scale=1.0 - lambda_init),
        grid=(b, DA_HEADS, s // tq),
        in_specs=[
            pl.BlockSpec(memory_space=pltpu.SMEM),
            pl.BlockSpec((None, tq, HEAD_DIM), lambda bi, h, i: (bi, i, base + h)),
            pl.BlockSpec((None, s, HEAD_DIM), lambda bi, h, i: (bi, 0, base + DA_HEADS + h)),
            pl.BlockSpec((None, s, HEAD_DIM), lambda bi, h, i: (bi, 0, base + 2 * DA_HEADS + h)),
            qtab, qtab, qtab, ktab, ktab, ktab,
            pl.BlockSpec((1, HEAD_DIM), lambda bi, h, i: (0, 0)),
        ],
        out_specs=pl.BlockSpec((None, tq, HEAD_DIM), lambda bi, h, i: (bi, i, h)),
        out_shape=jax.ShapeDtypeStruct((b, s, DA_HEADS * HEAD_DIM), BF16),
        scratch_shapes=[pltpu.VMEM((s, HEAD_DIM), BF16)],
        compiler_params=_cparams(("arbitrary", "arbitrary", "arbitrary")),
        name="da_attention",
    )(lam, proj, proj, proj, c, s_first, s_second, c, s_first, s_second, subln.reshape(1, HEAD_DIM))


SCAN_CHUNK = 512


def _shift_rows(x, k):
    return pltpu.roll(x, k % x.shape[0], axis=0)


def _block_scan(a, b, reverse):
    row = lax.broadcasted_iota(jnp.int32, a.shape, 0) % SUBLANES
    k = 1
    while k < SUBLANES:
        if reverse:
            keep = row < SUBLANES - k
            a_n, b_n = _shift_rows(a, -k), _shift_rows(b, -k)
        else:
            keep = row >= k
            a_n, b_n = _shift_rows(a, k), _shift_rows(b, k)
        b = jnp.where(keep, a * b_n + b, b)
        a = jnp.where(keep, a * a_n, a)
        k *= 2
    return a, b


def _lru_kernel(gate_ref, rec_ref, cw_ref, cb_ref, wa_ref, ba_ref, wx_ref, bx_ref, ap_ref, o_ref,
                u_ref, a_ref, b_ref, h_ref):
    s = rec_ref.shape[0]
    n_chunks = s // SCAN_CHUNK

    r = rec_ref[...].astype(F32)
    t_idx = lax.broadcasted_iota(jnp.int32, r.shape, 0)
    cw = cw_ref[...]
    u = r * cw[1:2, :] + cb_ref[...]
    u = u + jnp.where(t_idx >= 1, _shift_rows(r, 1), 0.0) * cw[0:1, :]
    u = u + jnp.where(t_idx < s - 1, _shift_rows(r, -1), 0.0) * cw[2:3, :]
    u = u + jnp.where(t_idx < s - 2, _shift_rows(r, -2), 0.0) * cw[3:4, :]
    u_ref[...] = u

    for d in range(2):
        reverse = d == 1
        ap = ap_ref[d]
        softplus = jnp.maximum(-ap, 0.0) + jnp.log1p(jnp.exp(-jnp.abs(ap)))
        wa = wa_ref[d].astype(BF16)
        wx = wx_ref[d].astype(BF16)

        def prep(c, carry, d=d, reverse=reverse, softplus=softplus, wa=wa, wx=wx):
            r0 = pl.multiple_of(c * SCAN_CHUNK, SCAN_CHUNK)
            uc = u_ref[pl.ds(r0, SCAN_CHUNK), :]
            ub = uc.astype(BF16)
            ga = jax.nn.sigmoid(jnp.dot(ub, wa, preferred_element_type=F32) + ba_ref[d])
            gx = jax.nn.sigmoid(jnp.dot(ub, wx, preferred_element_type=F32) + bx_ref[d])
            log_a = -LRU_C * ga * softplus
            a = jnp.exp(log_a)
            mult = jnp.sqrt(1.0 - a * a)
            a_s, b_s = _block_scan(a, mult * (gx * uc), reverse)
            a_ref[pl.ds(r0, SCAN_CHUNK), :] = a_s
            b_ref[pl.ds(r0, SCAN_CHUNK), :] = b_s
            return carry

        lax.fori_loop(0, n_chunks, prep, 0)

        n_groups = s // SUBLANES

        def step(i, h_prev, reverse=reverse):
            g = (n_groups - 1 - i) if reverse else i
            r0 = pl.multiple_of(g * SUBLANES, SUBLANES)
            h = a_ref[pl.ds(r0, SUBLANES), :] * h_prev + b_ref[pl.ds(r0, SUBLANES), :]
            if reverse:
                h_ref[pl.ds(r0, SUBLANES), :] = h_ref[pl.ds(r0, SUBLANES), :] + h
            else:
                h_ref[pl.ds(r0, SUBLANES), :] = h
            edge = h[0:1, :] if reverse else h[SUBLANES - 1:SUBLANES, :]
            return jnp.broadcast_to(edge, h.shape)

        lax.fori_loop(0, n_groups, step, jnp.zeros((SUBLANES, LRU_BLOCK_DIM), F32), unroll=8)

    y = jax.nn.gelu(gate_ref[...].astype(F32), approximate=True)
    o_ref[...] = (y * h_ref[...]).astype(o_ref.dtype)


def recurrent_block(proj, conv_w, conv_b, w_a, b_a, w_x, b_x, a_param):
    b, s, w2 = proj.shape
    w = w2 // 2
    nb = w // LRU_BLOCK_DIM
    c = LRU_BLOCK_DIM
    vec = lambda p: p.reshape(2, 1, w)
    return pl.pallas_call(
        _lru_kernel,
        grid=(b, nb),
        in_specs=[
            pl.BlockSpec((None, s, c), lambda bi, n: (bi, 0, n)),
            pl.BlockSpec((None, s, c), lambda bi, n: (bi, 0, nb + n)),
            pl.BlockSpec((conv_w.shape[0], c), lambda bi, n: (0, n)),
            pl.BlockSpec((1, c), lambda bi, n: (0, n)),
            pl.BlockSpec((2, None, c, c), lambda bi, n: (0, n, 0, 0)),
            pl.BlockSpec((2, 1, c), lambda bi, n: (0, 0, n)),
            pl.BlockSpec((2, None, c, c), lambda bi, n: (0, n, 0, 0)),
            pl.BlockSpec((2, 1, c), lambda bi, n: (0, 0, n)),
            pl.BlockSpec((2, 1, c), lambda bi, n: (0, 0, n)),
        ],
        out_specs=pl.BlockSpec((None, s, c), lambda bi, n: (bi, 0, n)),
        out_shape=jax.ShapeDtypeStruct((b, s, w), BF16),
        scratch_shapes=[pltpu.VMEM((s, c), F32) for _ in range(4)],
        compiler_params=_cparams(("arbitrary", "arbitrary")),
        name="rg_lru",
    )(proj, proj, conv_w, conv_b.reshape(1, w), w_a, vec(b_a), w_x, vec(b_x), vec(a_param))


R_E0, R_E1, R_G0, R_G1, R_RANK0, R_RANK1 = range(6)


def _route_kernel(x_ref, g_ref, r_ref, meta_ref, cnt_ref, carry_ref):
    i = pl.program_id(0)

    @pl.when(i == 0)
    def _():
        carry_ref[...] = jnp.zeros_like(carry_ref)

    h = _rms(x_ref[...], g_ref[...])
    logits = jnp.dot(h, r_ref[...], preferred_element_type=F32, precision=lax.Precision.HIGHEST)
    tm = logits.shape[0]
    lane = lax.broadcasted_iota(jnp.int32, logits.shape, 1).astype(F32)
    logits = jnp.where(lane < N_EXPERTS, logits, -jnp.inf)
    m0 = jnp.max(logits, axis=-1, keepdims=True)
    e0 = jnp.min(jnp.where(logits == m0, lane, float(LANES)), axis=-1, keepdims=True)
    rest = jnp.where(lane == e0, -jnp.inf, logits)
    m1 = jnp.max(rest, axis=-1, keepdims=True)
    e1 = jnp.min(jnp.where(rest == m1, lane, float(LANES)), axis=-1, keepdims=True)
    z = jnp.exp(m1 - m0)
    g0 = 1.0 / (1.0 + z)
    g1 = z / (1.0 + z)

    hit0 = lane == e0
    hit1 = lane == e1
    both = jnp.where(hit0 | hit1, 1.0, 0.0).astype(BF16)
    ri = lax.broadcasted_iota(jnp.int32, (tm, tm), 0)
    ci = lax.broadcasted_iota(jnp.int32, (tm, tm), 1)
    lower = jnp.where(ci < ri, 1.0, 0.0).astype(BF16)
    before = jnp.dot(lower, both, preferred_element_type=F32) + carry_ref[0:1, :]
    rank0 = jnp.sum(jnp.where(hit0, before, 0.0), axis=-1, keepdims=True)
    rank1 = jnp.sum(jnp.where(hit1, before, 0.0), axis=-1, keepdims=True)
    total = carry_ref[0:1, :] + jnp.sum(both.astype(F32), axis=0, keepdims=True)
    carry_ref[...] = jnp.broadcast_to(total, carry_ref.shape)
    cnt_ref[...] = jnp.broadcast_to(total, cnt_ref.shape)

    rec = jnp.zeros(logits.shape, F32)
    for col, val in ((R_E0, e0), (R_E1, e1), (R_G0, g0), (R_G1, g1),
                     (R_RANK0, rank0), (R_RANK1, rank1)):
        rec = jnp.where(lane == col, val, rec)
    meta_ref[...] = rec


def route(x, g, router):
    t, d = x.shape
    tm = ROUTE_TILE
    r_pad = jnp.zeros((d, LANES), F32).at[:, :N_EXPERTS].set(router)
    return pl.pallas_call(
        _route_kernel,
        grid=(t // tm,),
        in_specs=[pl.BlockSpec((tm, d), lambda i: (i, 0)),
                  pl.BlockSpec((1, d), lambda i: (0, 0)),
                  pl.BlockSpec((d, LANES), lambda i: (0, 0))],
        out_specs=[pl.BlockSpec((tm, LANES), lambda i: (i, 0)),
                   pl.BlockSpec((SUBLANES, LANES), lambda i: (0, 0))],
        out_shape=[jax.ShapeDtypeStruct((t, LANES), F32),
                   jax.ShapeDtypeStruct((SUBLANES, LANES), F32)],
        scratch_shapes=[pltpu.VMEM((SUBLANES, LANES), F32)],
        compiler_params=_cparams(("arbitrary",)),
        name="moe_route",
    )(x, g.reshape(1, d), r_pad)


def _row_copy(src_ref, src_row, dst_ref, dst_row, sem):
    return pltpu.make_async_copy(src_ref.at[pl.ds(src_row, 1), :], dst_ref.at[pl.ds(dst_row, 1), :], sem)


def _dispatch_kernel(p0_ref, p1_ref, x_ref, g_ref, init_ref, xs_ref, h_ref, sem):
    del init_ref
    tm = x_ref.shape[0]
    base = pl.program_id(0) * tm
    h_ref[...] = _rms(x_ref[...], g_ref[...])

    def start(t, carry):
        _row_copy(h_ref, t, xs_ref, p0_ref[base + t], sem).start()
        _row_copy(h_ref, t, xs_ref, p1_ref[base + t], sem).start()
        return carry

    lax.fori_loop(0, tm, start, 0)

    def wait(t, carry):
        _row_copy(h_ref, 0, xs_ref, 0, sem).wait()
        _row_copy(h_ref, 0, xs_ref, 0, sem).wait()
        return carry

    lax.fori_loop(0, tm, wait, 0)


def dispatch(x, g, pos0, pos1, n_rows):
    t, d = x.shape
    tm = ROUTE_TILE
    grid_spec = pltpu.PrefetchScalarGridSpec(
        num_scalar_prefetch=2,
        grid=(t // tm,),
        in_specs=[pl.BlockSpec((tm, d), lambda i, p0, p1: (i, 0)),
                  pl.BlockSpec((1, d), lambda i, p0, p1: (0, 0)),
                  pl.BlockSpec(memory_space=pl.ANY)],
        out_specs=pl.BlockSpec(memory_space=pl.ANY),
        scratch_shapes=[pltpu.VMEM((tm, d), F32), pltpu.SemaphoreType.DMA(())],
    )
    return pl.pallas_call(
        _dispatch_kernel,
        grid_spec=grid_spec,
        out_shape=jax.ShapeDtypeStruct((n_rows, d), F32),
        input_output_aliases={4: 0},
        compiler_params=_cparams(("arbitrary",)),
        name="moe_dispatch",
    )(pos0, pos1, x, g.reshape(1, d), jnp.zeros((n_rows, d), F32))


def _combine_kernel(p0_ref, p1_ref, x_ref, meta_ref, g_ref, y_ref, o_ref, y0_ref, y1_ref, sem):
    tm = x_ref.shape[0]
    base = pl.program_id(0) * tm

    def start(t, carry):
        _row_copy(y_ref, p0_ref[base + t], y0_ref, t, sem).start()
        _row_copy(y_ref, p1_ref[base + t], y1_ref, t, sem).start()
        return carry

    lax.fori_loop(0, tm, start, 0)

    def wait(t, carry):
        _row_copy(y_ref, 0, y0_ref, 0, sem).wait()
        _row_copy(y_ref, 0, y1_ref, 0, sem).wait()
        return carry

    lax.fori_loop(0, tm, wait, 0)
    meta = meta_ref[...]
    g0 = meta[:, R_G0:R_G0 + 1]
    g1 = meta[:, R_G1:R_G1 + 1]
    out = x_ref[...] + g0 * y0_ref[...] + g1 * y1_ref[...]
    o_ref[...] = _rms(out, g_ref[...])


def combine(x, meta, g, y, pos0, pos1):
    t, d = x.shape
    tm = ROUTE_TILE
    grid_spec = pltpu.PrefetchScalarGridSpec(
        num_scalar_prefetch=2,
        grid=(t // tm,),
        in_specs=[pl.BlockSpec((tm, d), lambda i, p0, p1: (i, 0)),
                  pl.BlockSpec((tm, LANES), lambda i, p0, p1: (i, 0)),
                  pl.BlockSpec((1, d), lambda i, p0, p1: (0, 0)),
                  pl.BlockSpec(memory_space=pl.ANY)],
        out_specs=pl.BlockSpec((tm, d), lambda i, p0, p1: (i, 0)),
        scratch_shapes=[pltpu.VMEM((tm, d), F32), pltpu.VMEM((tm, d), F32),
                        pltpu.SemaphoreType.DMA(())],
    )
    return pl.pallas_call(
        _combine_kernel,
        grid_spec=grid_spec,
        out_shape=jax.ShapeDtypeStruct((t, d), F32),
        compiler_params=_cparams(("arbitrary",)),
        name="moe_combine",
    )(pos0, pos1, x, meta, g.reshape(1, d), y)


def _moe_plan(meta, counts, n_tiles, tm):
    counts = counts[0, :N_EXPERTS].astype(jnp.int32)
    padded = ((counts + tm - 1) // tm) * tm
    ends = jnp.cumsum(padded)
    starts = ends - padded
    e0 = meta[:, R_E0].astype(jnp.int32)
    e1 = meta[:, R_E1].astype(jnp.int32)
    pos0 = starts[e0] + meta[:, R_RANK0].astype(jnp.int32)
    pos1 = starts[e1] + meta[:, R_RANK1].astype(jnp.int32)
    n_used = ends[-1] // tm
    tile_idx = jnp.arange(n_tiles, dtype=jnp.int32)
    last = jnp.maximum(n_used - 1, 0)
    owner = jnp.sum((jnp.minimum(tile_idx, last)[:, None] * tm >= ends[None, :]).astype(jnp.int32), axis=1)
    owner = jnp.minimum(owner, N_EXPERTS - 1)
    first = jnp.concatenate([jnp.ones((1,), jnp.int32),
                             (owner[1:] != owner[:-1]).astype(jnp.int32)])
    valid = (tile_idx < n_used).astype(jnp.int32)
    return pos0, pos1, (owner, first, valid)


def kernel(x, ev_mix_norm, ev_w_in, ev_na_rpb, ev_da_lambda_q1, ev_da_lambda_k1, ev_da_lambda_q2, ev_da_lambda_k2, ev_da_subln, ev_w_out, ev_ffn_norm, ev_ffn_w_gate, ev_ffn_w_up, ev_ffn_w_down, od_mix_norm, od_w_in, od_conv_w, od_conv_b, od_lru_w_a, od_lru_b_a, od_lru_w_x, od_lru_b_x, od_lru_a_param, od_w_out, od_ffn_norm, od_router, od_moe_w_gate, od_moe_w_up, od_moe_w_down, final_norm):
    b, s, d = x.shape
    t = b * s
    xt = x.reshape(t, d)

    lambda_init = 0.8 - 0.6 * math.exp(-0.3 * 0)
    h = rmsnorm(xt, ev_mix_norm[0], BF16)
    proj = ws_matmul(h, [ev_w_in], tm=512, tn=1024, epilogue="cast", out_dtype=BF16, name="l0_in_proj")
    proj = proj.reshape(b, s, -1)
    na_out = neighbourhood_attention(proj, ev_na_rpb[0])
    lam = (jnp.exp(jnp.sum(ev_da_lambda_q1[0] * ev_da_lambda_k1[0]))
           - jnp.exp(jnp.sum(ev_da_lambda_q2[0] * ev_da_lambda_k2[0])) + lambda_init).reshape(1)
    da_out = differential_attention(proj, lam, ev_da_subln[0], lambda_init)
    mix = jnp.concatenate([na_out, da_out], axis=-1).reshape(t, -1)
    xt = ws_matmul(mix, [ev_w_out], tm=512, tn=1024, epilogue="residual", out_dtype=F32, res=xt,
                   name="l0_out_proj")

    h = rmsnorm(xt, ev_ffn_norm[0], BF16)
    act = ws_matmul(h, [ev_ffn_w_gate, ev_ffn_w_up], tm=512, tn=1024, epilogue="swiglu",
                    out_dtype=BF16, name="l0_ffn_up")
    xt = ws_matmul(act, [ev_ffn_w_down], tm=256, tn=512, epilogue="residual", out_dtype=F32, res=xt,
                   name="l0_ffn_down")

    h = rmsnorm(xt, od_mix_norm[0], BF16)
    proj = ws_matmul(h, [od_w_in], tm=512, tn=1024, epilogue="cast", out_dtype=BF16, name="l1_in_proj")
    rec = recurrent_block(proj.reshape(b, s, -1), od_conv_w[0], od_conv_b[0], od_lru_w_a[0],
                          od_lru_b_a[0], od_lru_w_x[0], od_lru_b_x[0], od_lru_a_param[0])
    xt = ws_matmul(rec.reshape(t, -1), [od_w_out], tm=512, tn=1024, epilogue="residual",
                   out_dtype=F32, res=xt, name="l1_out_proj")

    tm = MOE_ROW_TILE
    n_tiles = (t * 2) // tm + N_EXPERTS
    meta, counts = route(xt, od_ffn_norm[0], od_router[0])
    pos0, pos1, tiles = _moe_plan(meta, counts, n_tiles, tm)
    xs = dispatch(xt, od_ffn_norm[0], pos0, pos1, n_tiles * tm)
    act = ws_matmul(xs, [od_moe_w_gate[0], od_moe_w_up[0]], tm=tm, tn=1024,
                    epilogue="swiglu", out_dtype=BF16, tiles=tiles, name="moe_up")
    y = ws_matmul(act, [od_moe_w_down[0]], tm=tm, tn=512, epilogue="cast", out_dtype=F32,
                  tiles=tiles, name="moe_down")
    out = combine(xt, meta, final_norm, y, pos0, pos1)
    return out.reshape(b, s, d)
```

```python
import functools
import math

import numpy as np
import jax
import jax.numpy as jnp
from jax import lax
from jax.experimental import pallas as pl
from jax.experimental.pallas import tpu as pltpu

F32 = jnp.float32
BF16 = jnp.bfloat16

HEAD_DIM = 128
NA_HEADS = 8
DA_HEADS = 8
GRID_W = 64
NA_KH = 8
NA_KW = 16
ROPE_THETA = 10000.0
LRU_BLOCK_DIM = 128
LRU_C = 8.0
N_EXPERTS = 8
EPS = 1e-6
NEG_INF = -1e30
LOG2E = math.log2(math.e)

NA_Q_ROWS = 8
NA_K_ROWS = NA_Q_ROWS + NA_KH
DA_Q_BLOCK = 256
MOE_ROW_TILE = 256
ROUTE_TILE = 256
LANES = 128
SUBLANES = 8
VMEM_LIMIT = 56 * 1024 * 1024


def _cparams(semantics):
    return pltpu.CompilerParams(dimension_semantics=semantics, vmem_limit_bytes=VMEM_LIMIT)


def _rms(x, g):
    return x * lax.rsqrt(jnp.mean(x * x, axis=-1, keepdims=True) + EPS) * g


def _rmsnorm_kernel(x_ref, g_ref, o_ref):
    o_ref[...] = _rms(x_ref[...], g_ref[...]).astype(o_ref.dtype)


def rmsnorm(x, g, out_dtype, tm=512):
    t, d = x.shape
    return pl.pallas_call(
        _rmsnorm_kernel,
        grid=(t // tm,),
        in_specs=[pl.BlockSpec((tm, d), lambda i: (i, 0)),
                  pl.BlockSpec((1, d), lambda i: (0, 0))],
        out_specs=pl.BlockSpec((tm, d), lambda i: (i, 0)),
        out_shape=jax.ShapeDtypeStruct((t, d), out_dtype),
        compiler_params=_cparams(("arbitrary",)),
        name="rmsnorm",
    )(x, g.reshape(1, d))


CAST_ROWS = 256


def _cast_weight(w_ref, wbf_ref):
    k = w_ref.shape[0]

    def body(c, carry):
        r = pl.multiple_of(c * CAST_ROWS, CAST_ROWS)
        wbf_ref[pl.ds(r, CAST_ROWS), :] = w_ref[pl.ds(r, CAST_ROWS), :].astype(BF16)
        return carry

    lax.fori_loop(0, k // CAST_ROWS, body, 0)


def _mm_kernel(a_ref, *refs, n_w, epilogue):
    w_refs = refs[:n_w]
    pos = n_w
    res_ref = None
    if epilogue == "residual":
        res_ref = refs[pos]
        pos += 1
    o_ref = refs[pos]
    wbf_refs = refs[pos + 1:pos + 1 + n_w]

    @pl.when(pl.program_id(1) == 0)
    def _():
        for w_ref, wbf_ref in zip(w_refs, wbf_refs):
            _cast_weight(w_ref, wbf_ref)

    a = a_ref[...].astype(BF16)
    acc = jnp.dot(a, wbf_refs[0][...], preferred_element_type=F32)
    if epilogue == "swiglu":
        up = jnp.dot(a, wbf_refs[1][...], preferred_element_type=F32)
        acc = acc * jax.nn.sigmoid(acc) * up
    elif epilogue == "residual":
        acc = acc + res_ref[...]
    o_ref[...] = acc.astype(o_ref.dtype)


def ws_matmul(a, ws, *, tm, tn, epilogue, out_dtype, res=None, name):
    m, k = a.shape
    n = ws[0].shape[-1]
    n_w = len(ws)
    in_specs = [pl.BlockSpec((tm, k), lambda j, i: (i, 0))]
    in_specs += [pl.BlockSpec((None, k, tn), lambda j, i: (0, 0, j)) for _ in ws]
    operands = [a, *ws]
    if epilogue == "residual":
        in_specs.append(pl.BlockSpec((tm, tn), lambda j, i: (i, j)))
        operands.append(res)
    return pl.pallas_call(
        functools.partial(_mm_kernel, n_w=n_w, epilogue=epilogue),
        grid=(n // tn, m // tm),
        in_specs=in_specs,
        out_specs=pl.BlockSpec((tm, tn), lambda j, i: (i, j)),
        out_shape=jax.ShapeDtypeStruct((m, n), out_dtype),
        scratch_shapes=[pltpu.VMEM((k, tn), BF16) for _ in ws],
        compiler_params=_cparams(("arbitrary", "arbitrary")),
        name=name,
    )(*operands)


def _gmm_kernel(gs_ref, gn_ref, a_hbm, *refs, n_w, epilogue, tm, tn, n_tiles):
    w_refs = refs[:n_w]
    o_hbm = refs[n_w]
    wbf_refs = refs[n_w + 1:2 * n_w + 1]
    abuf, obuf, a_sem, o_sem = refs[2 * n_w + 1:]
    j = pl.program_id(0)
    e = pl.program_id(1)
    n = gn_ref[e]
    t0 = gs_ref[e]
    col = pl.multiple_of(j * tn, tn)

    def a_copy(t, slot):
        row = pl.multiple_of((t0 + t) * tm, tm)
        return pltpu.make_async_copy(a_hbm.at[pl.ds(row, tm), :], abuf.at[slot], a_sem.at[slot])

    def o_copy(tile, slot):
        row = pl.multiple_of(tile * tm, tm)
        return pltpu.make_async_copy(obuf.at[slot], o_hbm.at[pl.ds(row, tm), pl.ds(col, tn)],
                                     o_sem.at[slot])

    @pl.when(n > 0)
    def _():
        a_copy(0, 0).start()
        for w_ref, wbf_ref in zip(w_refs, wbf_refs):
            _cast_weight(w_ref, wbf_ref)

        def body(t, carry):
            slot = lax.rem(t, 2)

            @pl.when(t + 1 < n)
            def _():
                a_copy(t + 1, 1 - slot).start()

            a_copy(t, slot).wait()

            @pl.when(t >= 2)
            def _():
                o_copy(t0 + t - 2, slot).wait()

            a = abuf[slot].astype(BF16)
            acc = jnp.dot(a, wbf_refs[0][...], preferred_element_type=F32)
            if epilogue == "swiglu":
                up = jnp.dot(a, wbf_refs[1][...], preferred_element_type=F32)
                acc = acc * jax.nn.sigmoid(acc) * up
            obuf[slot] = acc.astype(obuf.dtype)
            o_copy(t0 + t, slot).start()
            return carry

        lax.fori_loop(0, n, body, 0)

        @pl.when(n >= 2)
        def _():
            o_copy(t0 + n - 2, lax.rem(n, 2)).wait()

        o_copy(t0 + n - 1, lax.rem(n - 1, 2)).wait()

    @pl.when(e == pl.num_programs(1) - 1)
    def _():
        used = t0 + n
        obuf[0] = jnp.zeros(obuf.shape[1:], obuf.dtype)

        def fill(tile, carry):
            o_copy(tile, 0).start()
            o_copy(tile, 0).wait()
            return carry

        lax.fori_loop(used, n_tiles, fill, 0)


def grouped_matmul(a, ws, group_start, group_tiles, *, tm, tn, epilogue, out_dtype, name):
    m, k = a.shape
    n_e, _, n = ws[0].shape
    n_w = len(ws)
    grid_spec = pltpu.PrefetchScalarGridSpec(
        num_scalar_prefetch=2,
        grid=(n // tn, n_e),
        in_specs=[pl.BlockSpec(memory_space=pl.ANY)]
        + [pl.BlockSpec((None, k, tn), lambda j, e, gs, gn: (e, 0, j)) for _ in ws],
        out_specs=pl.BlockSpec(memory_space=pl.ANY),
        scratch_shapes=[pltpu.VMEM((k, tn), BF16) for _ in ws]
        + [pltpu.VMEM((2, tm, k), a.dtype), pltpu.VMEM((2, tm, tn), out_dtype),
           pltpu.SemaphoreType.DMA((2,)), pltpu.SemaphoreType.DMA((2,))],
    )
    return pl.pallas_call(
        functools.partial(_gmm_kernel, n_w=n_w, epilogue=epilogue, tm=tm, tn=tn, n_tiles=m // tm),
        grid_spec=grid_spec,
        out_shape=jax.ShapeDtypeStruct((m, n), out_dtype),
        compiler_params=_cparams(("arbitrary", "arbitrary")),
        name=name,
    )(group_start, group_tiles, a, *ws)


def _na_bias_tiles(rpb, rows):
    n_heads, _, n_co = rpb.shape
    c = np.arange(GRID_W)
    c_start = np.clip(c - NA_KW // 2, 0, GRID_W - NA_KW)
    col_in = (c[None, :] >= c_start[:, None]) & (c[None, :] < c_start[:, None] + NA_KW)
    col_off = np.clip(c[None, :] - c[:, None], -(NA_KW - 1), NA_KW - 1) + NA_KW - 1
    pick = (col_off[None] == np.arange(n_co)[:, None, None]).astype(np.float32)
    blocks = jnp.einsum("hrj,jqk->hrqk", rpb.astype(F32) * LOG2E, pick,
                        precision=lax.Precision.HIGHEST)
    blocks = jnp.where(col_in[None, None], blocks, NEG_INF)
    masked = jnp.full((n_heads, GRID_W, GRID_W), NEG_INF, F32)
    tiles = []
    for r0, ks in ((0, 0), (NA_Q_ROWS, NA_Q_ROWS - NA_KH // 2), (rows - NA_Q_ROWS, rows - NA_K_ROWS)):
        q_rows = []
        for qrow in range(r0, r0 + NA_Q_ROWS):
            r_start = min(max(qrow - NA_KH // 2, 0), rows - NA_KH)
            q_rows.append(jnp.concatenate(
                [blocks[:, krow - qrow + NA_KH - 1] if r_start <= krow < r_start + NA_KH else masked
                 for krow in range(ks, ks + NA_K_ROWS)], axis=-1))
        tiles.append(jnp.concatenate(q_rows, axis=1))
    return jnp.stack(tiles, axis=1)


def _na_kernel(q_ref, k_ref, v_ref, b_ref, o_ref, *, rows):
    rb = pl.program_id(2)
    ks = jnp.clip(rb * NA_Q_ROWS - NA_KH // 2, 0, rows - NA_K_ROWS)
    k0 = pl.multiple_of(ks * GRID_W, GRID_W)
    nk = NA_K_ROWS * GRID_W
    kb = k_ref[pl.ds(k0, nk), :]
    vb = v_ref[pl.ds(k0, nk), :]
    s = lax.dot_general(q_ref[...], kb, (((1,), (1,)), ((), ())), preferred_element_type=F32)
    s = s * (HEAD_DIM ** -0.5 * LOG2E) + b_ref[...]
    p = jnp.exp2(s - jnp.max(s, axis=-1, keepdims=True)).astype(BF16)
    v1 = jnp.concatenate([vb, jnp.ones_like(vb)], axis=-1)
    o = jnp.dot(p, v1, preferred_element_type=F32)
    o_ref[...] = (o[:, :HEAD_DIM] / o[:, HEAD_DIM:]).astype(o_ref.dtype)


def neighbourhood_attention(proj, rpb):
    b, s, _ = proj.shape
    rows = s // GRID_W
    n_rb = rows // NA_Q_ROWS
    tq = NA_Q_ROWS * GRID_W
    bias = _na_bias_tiles(rpb, rows)

    def pat(r):
        return jnp.where(r == 0, 0, jnp.where(r == n_rb - 1, 2, 1))

    return pl.pallas_call(
        functools.partial(_na_kernel, rows=rows),
        grid=(b, NA_HEADS, n_rb),
        in_specs=[
            pl.BlockSpec((None, tq, HEAD_DIM), lambda bi, h, r: (bi, r, h)),
            pl.BlockSpec((None, s, HEAD_DIM), lambda bi, h, r: (bi, 0, NA_HEADS + h)),
            pl.BlockSpec((None, s, HEAD_DIM), lambda bi, h, r: (bi, 0, 2 * NA_HEADS + h)),
            pl.BlockSpec((None, None, tq, NA_K_ROWS * GRID_W), lambda bi, h, r: (h, pat(r), 0, 0)),
        ],
        out_specs=pl.BlockSpec((None, tq, HEAD_DIM), lambda bi, h, r: (bi, r, h)),
        out_shape=jax.ShapeDtypeStruct((b, s, NA_HEADS * HEAD_DIM), BF16),
        compiler_params=_cparams(("arbitrary", "arbitrary", "arbitrary")),
        name="na_attention",
    )(proj, proj, proj, bias)


def _rope_tables(seq):
    dq = HEAD_DIM // 2
    inv = 1.0 / (ROPE_THETA ** (jnp.arange(0, dq, 2, dtype=F32) / dq))
    ang = jnp.arange(seq, dtype=F32)[:, None] * inv[None, :]
    cos, sin = jnp.cos(ang), jnp.sin(ang)
    zero = jnp.zeros_like(sin)
    c = jnp.concatenate([cos, cos, cos, cos], axis=-1)
    s_first = jnp.concatenate([-sin, zero, -sin, zero], axis=-1)
    s_second = jnp.concatenate([zero, sin, zero, sin], axis=-1)
    return c, s_first, s_second


def _rope(x, c, s_first, s_second):
    half = HEAD_DIM // 4
    return (x * c + pltpu.roll(x, HEAD_DIM - half, axis=1) * s_first
            + pltpu.roll(x, half, axis=1) * s_second)


def _da_kernel(lam_ref, q_ref, k_ref, v_ref, cq_ref, s1q_ref, s2q_ref, ck_ref, s1k_ref, s2k_ref,
               g_ref, o_ref, kr_ref, *, out_scale):
    @pl.when(pl.program_id(2) == 0)
    def _():
        k = k_ref[...].astype(F32)
        kr_ref[...] = _rope(k, ck_ref[...], s1k_ref[...], s2k_ref[...]).astype(BF16)

    dq = HEAD_DIM // 2
    q = _rope(q_ref[...].astype(F32), cq_ref[...], s1q_ref[...], s2q_ref[...]) * (dq ** -0.5 * LOG2E)
    lane = lax.broadcasted_iota(jnp.int32, q.shape, 1)
    q1 = jnp.where(lane < dq, q, 0.0).astype(BF16)
    q2 = jnp.where(lane >= dq, q, 0.0).astype(BF16)
    kr = kr_ref[...]
    dn = (((1,), (1,)), ((), ()))
    s1 = lax.dot_general(q1, kr, dn, preferred_element_type=F32)
    s2 = lax.dot_general(q2, kr, dn, preferred_element_type=F32)
    p1 = jnp.exp2(s1 - jnp.max(s1, axis=-1, keepdims=True))
    p2 = jnp.exp2(s2 - jnp.max(s2, axis=-1, keepdims=True))
    w1 = 1.0 / jnp.sum(p1, axis=-1, keepdims=True)
    w2 = lam_ref[0] / jnp.sum(p2, axis=-1, keepdims=True)
    attn = (p1 * w1 - p2 * w2).astype(BF16)
    o = jnp.dot(attn, v_ref[...], preferred_element_type=F32)
    o_ref[...] = (_rms(o, g_ref[...]) * out_scale).astype(o_ref.dtype)


def differential_attention(proj, lam, subln, lambda_init):
    b, s, _ = proj.shape
    tq = min(DA_Q_BLOCK, s)
    base = 3 * NA_HEADS
    c, s_first, s_second = _rope_tables(s)
    qtab = pl.BlockSpec((tq, HEAD_DIM), lambda bi, h, i: (i, 0))
    ktab = pl.BlockSpec((s, HEAD_DIM), lambda bi, h, i: (0, 0))
    return pl.pallas_call(
        functools.partial(_da_kernel, out_scale=1.0 - lambda_init),
        grid=(b, DA_HEADS, s // tq),
        in_specs=[
            pl.BlockSpec(memory_space=pltpu.SMEM),
            pl.BlockSpec((None, tq, HEAD_DIM), lambda bi, h, i: (bi, i, base + h)),
            pl.BlockSpec((None, s, HEAD_DIM), lambda bi, h, i: (bi, 0, base + DA_HEADS + h)),
            pl.BlockSpec((None, s, HEAD_DIM), lambda bi, h, i: (bi, 0, base + 2 * DA_HEADS + h)),
            qtab, qtab, qtab, ktab, ktab, ktab,
            pl.BlockSpec((1, HEAD_DIM), lambda bi, h, i: (0, 0)),
        ],
        out_specs=pl.BlockSpec((None, tq, HEAD_DIM), lambda bi, h, i: (bi, i, h)),
        out_shape=jax.ShapeDtypeStruct((b, s, DA_HEADS * HEAD_DIM), BF16),
        scratch_shapes=[pltpu.VMEM((s, HEAD_DIM), BF16)],
        compiler_params=_cparams(("arbitrary", "arbitrary", "arbitrary")),
        name="da_attention",
    )(lam, proj, proj, proj, c, s_first, s_second, c, s_first, s_second, subln.reshape(1, HEAD_DIM))


SCAN_CHUNK = 512


def _shift_rows(x, k):
    return pltpu.roll(x, k % x.shape[0], axis=0)


def _block_scan(a, b, reverse):
    row = lax.broadcasted_iota(jnp.int32, a.shape, 0) % SUBLANES
    k = 1
    while k < SUBLANES:
        if reverse:
            keep = row < SUBLANES - k
            a_n, b_n = _shift_rows(a, -k), _shift_rows(b, -k)
        else:
            keep = row >= k
            a_n, b_n = _shift_rows(a, k), _shift_rows(b, k)
        b = jnp.where(keep, a * b_n + b, b)
        a = jnp.where(keep, a * a_n, a)
        k *= 2
    return a, b


def _lru_kernel(gate_ref, rec_ref, cw_ref, cb_ref, wa_ref, ba_ref, wx_ref, bx_ref, ap_ref, o_ref,
                u_ref, a_ref, b_ref, h_ref):
    s = rec_ref.shape[0]
    n_chunks = s // SCAN_CHUNK

    r = rec_ref[...].astype(F32)
    t_idx = lax.broadcasted_iota(jnp.int32, r.shape, 0)
    cw = cw_ref[...]
    u = r * cw[1:2, :] + cb_ref[...]
    u = u + jnp.where(t_idx >= 1, _shift_rows(r, 1), 0.0) * cw[0:1, :]
    u = u + jnp.where(t_idx < s - 1, _shift_rows(r, -1), 0.0) * cw[2:3, :]
    u = u + jnp.where(t_idx < s - 2, _shift_rows(r, -2), 0.0) * cw[3:4, :]
    u_ref[...] = u

    for d in range(2):
        reverse = d == 1
        ap = ap_ref[d]
        softplus = jnp.maximum(-ap, 0.0) + jnp.log1p(jnp.exp(-jnp.abs(ap)))
        wa = wa_ref[d].astype(BF16)
        wx = wx_ref[d].astype(BF16)

        def prep(c, carry, d=d, reverse=reverse, softplus=softplus, wa=wa, wx=wx):
            r0 = pl.multiple_of(c * SCAN_CHUNK, SCAN_CHUNK)
            uc = u_ref[pl.ds(r0, SCAN_CHUNK), :]
            ub = uc.astype(BF16)
            ga = jax.nn.sigmoid(jnp.dot(ub, wa, preferred_element_type=F32) + ba_ref[d])
            gx = jax.nn.sigmoid(jnp.dot(ub, wx, preferred_element_type=F32) + bx_ref[d])
            log_a = -LRU_C * ga * softplus
            a = jnp.exp(log_a)
            mult = jnp.sqrt(1.0 - a * a)
            a_s, b_s = _block_scan(a, mult * (gx * uc), reverse)
            a_ref[pl.ds(r0, SCAN_CHUNK), :] = a_s
            b_ref[pl.ds(r0, SCAN_CHUNK), :] = b_s
            return carry

        lax.fori_loop(0, n_chunks, prep, 0)

        n_groups = s // SUBLANES

        def step(i, h_prev, reverse=reverse):
            g = (n_groups - 1 - i) if reverse else i
            r0 = pl.multiple_of(g * SUBLANES, SUBLANES)
            h = a_ref[pl.ds(r0, SUBLANES), :] * h_prev + b_ref[pl.ds(r0, SUBLANES), :]
            if reverse:
                h_ref[pl.ds(r0, SUBLANES), :] = h_ref[pl.ds(r0, SUBLANES), :] + h
            else:
                h_ref[pl.ds(r0, SUBLANES), :] = h
            edge = h[0:1, :] if reverse else h[SUBLANES - 1:SUBLANES, :]
            return jnp.broadcast_to(edge, h.shape)

        lax.fori_loop(0, n_groups, step, jnp.zeros((SUBLANES, LRU_BLOCK_DIM), F32), unroll=8)

    y = jax.nn.gelu(gate_ref[...].astype(F32), approximate=True)
    o_ref[...] = (y * h_ref[...]).astype(o_ref.dtype)


def recurrent_block(proj, conv_w, conv_b, w_a, b_a, w_x, b_x, a_param):
    b, s, w2 = proj.shape
    w = w2 // 2
    nb = w // LRU_BLOCK_DIM
    c = LRU_BLOCK_DIM
    vec = lambda p: p.reshape(2, 1, w)
    return pl.pallas_call(
        _lru_kernel,
        grid=(b, nb),
        in_specs=[
            pl.BlockSpec((None, s, c), lambda bi, n: (bi, 0, n)),
            pl.BlockSpec((None, s, c), lambda bi, n: (bi, 0, nb + n)),
            pl.BlockSpec((conv_w.shape[0], c), lambda bi, n: (0, n)),
            pl.BlockSpec((1, c), lambda bi, n: (0, n)),
            pl.BlockSpec((2, None, c, c), lambda bi, n: (0, n, 0, 0)),
            pl.BlockSpec((2, 1, c), lambda bi, n: (0, 0, n)),
            pl.BlockSpec((2, None, c, c), lambda bi, n: (0, n, 0, 0)),
            pl.BlockSpec((2, 1, c), lambda bi, n: (0, 0, n)),
            pl.BlockSpec((2, 1, c), lambda bi, n: (0, 0, n)),
        ],
        out_specs=pl.BlockSpec((None, s, c), lambda bi, n: (bi, 0, n)),
        out_shape=jax.ShapeDtypeStruct((b, s, w), BF16),
        scratch_shapes=[pltpu.VMEM((s, c), F32) for _ in range(4)],
        compiler_params=_cparams(("arbitrary", "arbitrary")),
        name="rg_lru",
    )(proj, proj, conv_w, conv_b.reshape(1, w), w_a, vec(b_a), w_x, vec(b_x), vec(a_param))


R_E0, R_E1, R_G0, R_G1, R_RANK0, R_RANK1 = range(6)


def _route_kernel(x_ref, g_ref, r_ref, meta_ref, cnt_ref, carry_ref):
    i = pl.program_id(0)

    @pl.when(i == 0)
    def _():
        carry_ref[...] = jnp.zeros_like(carry_ref)

    h = _rms(x_ref[...], g_ref[...])
    logits = jnp.dot(h, r_ref[...], preferred_element_type=F32, precision=lax.Precision.HIGHEST)
    tm = logits.shape[0]
    lane = lax.broadcasted_iota(jnp.int32, logits.shape, 1).astype(F32)
    logits = jnp.where(lane < N_EXPERTS, logits, -jnp.inf)
    m0 = jnp.max(logits, axis=-1, keepdims=True)
    e0 = jnp.min(jnp.where(logits == m0, lane, float(LANES)), axis=-1, keepdims=True)
    rest = jnp.where(lane == e0, -jnp.inf, logits)
    m1 = jnp.max(rest, axis=-1, keepdims=True)
    e1 = jnp.min(jnp.where(rest == m1, lane, float(LANES)), axis=-1, keepdims=True)
    z = jnp.exp(m1 - m0)
    g0 = 1.0 / (1.0 + z)
    g1 = z / (1.0 + z)

    hit0 = lane == e0
    hit1 = lane == e1
    both = jnp.where(hit0 | hit1, 1.0, 0.0).astype(BF16)
    ri = lax.broadcasted_iota(jnp.int32, (tm, tm), 0)
    ci = lax.broadcasted_iota(jnp.int32, (tm, tm), 1)
    lower = jnp.where(ci < ri, 1.0, 0.0).astype(BF16)
    before = jnp.dot(lower, both, preferred_element_type=F32) + carry_ref[0:1, :]
    rank0 = jnp.sum(jnp.where(hit0, before, 0.0), axis=-1, keepdims=True)
    rank1 = jnp.sum(jnp.where(hit1, before, 0.0), axis=-1, keepdims=True)
    total = carry_ref[0:1, :] + jnp.sum(both.astype(F32), axis=0, keepdims=True)
    carry_ref[...] = jnp.broadcast_to(total, carry_ref.shape)
    cnt_ref[...] = jnp.broadcast_to(total, cnt_ref.shape)

    rec = jnp.zeros(logits.shape, F32)
    for col, val in ((R_E0, e0), (R_E1, e1), (R_G0, g0), (R_G1, g1),
                     (R_RANK0, rank0), (R_RANK1, rank1)):
        rec = jnp.where(lane == col, val, rec)
    meta_ref[...] = rec


def route(x, g, router):
    t, d = x.shape
    tm = ROUTE_TILE
    r_pad = jnp.zeros((d, LANES), F32).at[:, :N_EXPERTS].set(router)
    return pl.pallas_call(
        _route_kernel,
        grid=(t // tm,),
        in_specs=[pl.BlockSpec((tm, d), lambda i: (i, 0)),
                  pl.BlockSpec((1, d), lambda i: (0, 0)),
                  pl.BlockSpec((d, LANES), lambda i: (0, 0))],
        out_specs=[pl.BlockSpec((tm, LANES), lambda i: (i, 0)),
                   pl.BlockSpec((SUBLANES, LANES), lambda i: (0, 0))],
        out_shape=[jax.ShapeDtypeStruct((t, LANES), F32),
                   jax.ShapeDtypeStruct((SUBLANES, LANES), F32)],
        scratch_shapes=[pltpu.VMEM((SUBLANES, LANES), F32)],
        compiler_params=_cparams(("arbitrary",)),
        name="moe_route",
    )(x, g.reshape(1, d), r_pad)


def _row_copy(src_ref, src_row, dst_ref, dst_row, sem):
    return pltpu.make_async_copy(src_ref.at[pl.ds(src_row, 1), :], dst_ref.at[pl.ds(dst_row, 1), :], sem)


def _dispatch_kernel(p0_ref, p1_ref, x_ref, g_ref, init_ref, xs_ref, h_ref, sem):
    del init_ref
    tm = x_ref.shape[0]
    base = pl.program_id(0) * tm
    h_ref[...] = _rms(x_ref[...], g_ref[...])

    def start(t, carry):
        _row_copy(h_ref, t, xs_ref, p0_ref[base + t], sem).start()
        _row_copy(h_ref, t, xs_ref, p1_ref[base + t], sem).start()
        return carry

    lax.fori_loop(0, tm, start, 0)

    def wait(t, carry):
        _row_copy(h_ref, 0, xs_ref, 0, sem).wait()
        _row_copy(h_ref, 0, xs_ref, 0, sem).wait()
        return carry

    lax.fori_loop(0, tm, wait, 0)


def dispatch(x, g, pos0, pos1, n_rows):
    t, d = x.shape
    tm = ROUTE_TILE
    grid_spec = pltpu.PrefetchScalarGridSpec(
        num_scalar_prefetch=2,
        grid=(t // tm,),
        in_specs=[pl.BlockSpec((tm, d), lambda i, p0, p1: (i, 0)),
                  pl.BlockSpec((1, d), lambda i, p0, p1: (0, 0)),
                  pl.BlockSpec(memory_space=pl.ANY)],
        out_specs=pl.BlockSpec(memory_space=pl.ANY),
        scratch_shapes=[pltpu.VMEM((tm, d), F32), pltpu.SemaphoreType.DMA(())],
    )
    return pl.pallas_call(
        _dispatch_kernel,
        grid_spec=grid_spec,
        out_shape=jax.ShapeDtypeStruct((n_rows, d), F32),
        input_output_aliases={4: 0},
        compiler_params=_cparams(("arbitrary",)),
        name="moe_dispatch",
    )(pos0, pos1, x, g.reshape(1, d), jnp.zeros((n_rows, d), F32))


def _combine_kernel(p0_ref, p1_ref, x_ref, meta_ref, g_ref, y_ref, o_ref, y0_ref, y1_ref, sem):
    tm = x_ref.shape[0]
    base = pl.program_id(0) * tm

    def start(t, carry):
        _row_copy(y_ref, p0_ref[base + t], y0_ref, t, sem).start()
        _row_copy(y_ref, p1_ref[base + t], y1_ref, t, sem).start()
        return carry

    lax.fori_loop(0, tm, start, 0)

    def wait(t, carry):
        _row_copy(y_ref, 0, y0_ref, 0, sem).wait()
        _row_copy(y_ref, 0, y1_ref, 0, sem).wait()
        return carry

    lax.fori_loop(0, tm, wait, 0)
    meta = meta_ref[...]
    g0 = meta[:, R_G0:R_G0 + 1]
    g1 = meta[:, R_G1:R_G1 + 1]
    out = x_ref[...] + g0 * y0_ref[...] + g1 * y1_ref[...]
    o_ref[...] = _rms(out, g_ref[...])


def combine(x, meta, g, y, pos0, pos1):
    t, d = x.shape
    tm = ROUTE_TILE
    grid_spec = pltpu.PrefetchScalarGridSpec(
        num_scalar_prefetch=2,
        grid=(t // tm,),
        in_specs=[pl.BlockSpec((tm, d), lambda i, p0, p1: (i, 0)),
                  pl.BlockSpec((tm, LANES), lambda i, p0, p1: (i, 0)),
                  pl.BlockSpec((1, d), lambda i, p0, p1: (0, 0)),
                  pl.BlockSpec(memory_space=pl.ANY)],
        out_specs=pl.BlockSpec((tm, d), lambda i, p0, p1: (i, 0)),
        scratch_shapes=[pltpu.VMEM((tm, d), F32), pltpu.VMEM((tm, d), F32),
                        pltpu.SemaphoreType.DMA(())],
    )
    return pl.pallas_call(
        _combine_kernel,
        grid_spec=grid_spec,
        out_shape=jax.ShapeDtypeStruct((t, d), F32),
        compiler_params=_cparams(("arbitrary",)),
        name="moe_combine",
    )(pos0, pos1, x, meta, g.reshape(1, d), y)


def _moe_plan(meta, counts, tm):
    counts = counts[0, :N_EXPERTS].astype(jnp.int32)
    group_tiles = (counts + tm - 1) // tm
    group_start = jnp.cumsum(group_tiles) - group_tiles
    e0 = meta[:, R_E0].astype(jnp.int32)
    e1 = meta[:, R_E1].astype(jnp.int32)
    pos0 = group_start[e0] * tm + meta[:, R_RANK0].astype(jnp.int32)
    pos1 = group_start[e1] * tm + meta[:, R_RANK1].astype(jnp.int32)
    return pos0, pos1, group_start, group_tiles


def kernel(x, ev_mix_norm, ev_w_in, ev_na_rpb, ev_da_lambda_q1, ev_da_lambda_k1, ev_da_lambda_q2, ev_da_lambda_k2, ev_da_subln, ev_w_out, ev_ffn_norm, ev_ffn_w_gate, ev_ffn_w_up, ev_ffn_w_down, od_mix_norm, od_w_in, od_conv_w, od_conv_b, od_lru_w_a, od_lru_b_a, od_lru_w_x, od_lru_b_x, od_lru_a_param, od_w_out, od_ffn_norm, od_router, od_moe_w_gate, od_moe_w_up, od_moe_w_down, final_norm):
    b, s, d = x.shape
    t = b * s
    xt = x.reshape(t, d)

    lambda_init = 0.8 - 0.6 * math.exp(-0.3 * 0)
    h = rmsnorm(xt, ev_mix_norm[0], BF16)
    proj = ws_matmul(h, [ev_w_in], tm=512, tn=1024, epilogue="cast", out_dtype=BF16, name="l0_in_proj")
    proj = proj.reshape(b, s, -1)
    na_out = neighbourhood_attention(proj, ev_na_rpb[0])
    lam = (jnp.exp(jnp.sum(ev_da_lambda_q1[0] * ev_da_lambda_k1[0]))
           - jnp.exp(jnp.sum(ev_da_lambda_q2[0] * ev_da_lambda_k2[0])) + lambda_init).reshape(1)
    da_out = differential_attention(proj, lam, ev_da_subln[0], lambda_init)
    mix = jnp.concatenate([na_out, da_out], axis=-1).reshape(t, -1)
    xt = ws_matmul(mix, [ev_w_out], tm=512, tn=1024, epilogue="residual", out_dtype=F32, res=xt,
                   name="l0_out_proj")

    h = rmsnorm(xt, ev_ffn_norm[0], BF16)
    act = ws_matmul(h, [ev_ffn_w_gate, ev_ffn_w_up], tm=512, tn=1024, epilogue="swiglu",
                    out_dtype=BF16, name="l0_ffn_up")
    xt = ws_matmul(act, [ev_ffn_w_down], tm=256, tn=512, epilogue="residual", out_dtype=F32, res=xt,
                   name="l0_ffn_down")

    h = rmsnorm(xt, od_mix_norm[0], BF16)
    proj = ws_matmul(h, [od_w_in], tm=512, tn=1024, epilogue="cast", out_dtype=BF16, name="l1_in_proj")
    rec = recurrent_block(proj.reshape(b, s, -1), od_conv_w[0], od_conv_b[0], od_lru_w_a[0],
                          od_lru_b_a[0], od_lru_w_x[0], od_lru_b_x[0], od_lru_a_param[0])
    xt = ws_matmul(rec.reshape(t, -1), [od_w_out], tm=512, tn=1024, epilogue="residual",
                   out_dtype=F32, res=xt, name="l1_out_proj")

    tm = MOE_ROW_TILE
    n_tiles = (t * 2) // tm + N_EXPERTS
    meta, counts = route(xt, od_ffn_norm[0], od_router[0])
    pos0, pos1, group_start, group_tiles = _moe_plan(meta, counts, tm)
    xs = dispatch(xt, od_ffn_norm[0], pos0, pos1, n_tiles * tm)
    act = grouped_matmul(xs, [od_moe_w_gate[0], od_moe_w_up[0]], group_start, group_tiles, tm=tm,
                         tn=1024, epilogue="swiglu", out_dtype=BF16, name="moe_up")
    y = grouped_matmul(act, [od_moe_w_down[0]], group_start, group_tiles, tm=tm, tn=512,
                       epilogue="cast", out_dtype=F32, name="moe_down")
    out = combine(xt, meta, final_norm, y, pos0, pos1)
    return out.reshape(b, s, d)
```

```python
import functools
import math

import numpy as np
import jax
import jax.numpy as jnp
from jax import lax
from jax.experimental import pallas as pl
from jax.experimental.pallas import tpu as pltpu

F32 = jnp.float32
BF16 = jnp.bfloat16

HEAD_DIM = 128
NA_HEADS = 8
DA_HEADS = 8
GRID_W = 64
NA_KH = 8
NA_KW = 16
ROPE_THETA = 10000.0
LRU_BLOCK_DIM = 128
LRU_C = 8.0
N_EXPERTS = 8
EPS = 1e-6
NEG_INF = -1e30
LOG2E = math.log2(math.e)

NA_Q_ROWS = 8
NA_K_ROWS = NA_Q_ROWS + NA_KH
DA_Q_BLOCK = 256
MOE_ROW_TILE = 256
MOE_CHUNK_TILES = 9
FFN_COL_TILE = 256
ROUTE_TILE = 256
LANES = 128
SUBLANES = 8
VMEM_LIMIT = 56 * 1024 * 1024


def _cparams(semantics):
    return pltpu.CompilerParams(dimension_semantics=semantics, vmem_limit_bytes=VMEM_LIMIT)


def _rms(x, g):
    return x * lax.rsqrt(jnp.mean(x * x, axis=-1, keepdims=True) + EPS) * g


def _rmsnorm_kernel(x_ref, g_ref, o_ref):
    o_ref[...] = _rms(x_ref[...], g_ref[...]).astype(o_ref.dtype)


def rmsnorm(x, g, out_dtype, tm=512):
    t, d = x.shape
    return pl.pallas_call(
        _rmsnorm_kernel,
        grid=(t // tm,),
        in_specs=[pl.BlockSpec((tm, d), lambda i: (i, 0)),
                  pl.BlockSpec((1, d), lambda i: (0, 0))],
        out_specs=pl.BlockSpec((tm, d), lambda i: (i, 0)),
        out_shape=jax.ShapeDtypeStruct((t, d), out_dtype),
        compiler_params=_cparams(("arbitrary",)),
        name="rmsnorm",
    )(x, g.reshape(1, d))


CAST_ROWS = 256


def _cast_weight(w_ref, wbf_ref):
    k = w_ref.shape[0]
    rows = min(CAST_ROWS, k)
    assert k % rows == 0

    def body(c, carry):
        r = pl.multiple_of(c * rows, rows)
        wbf_ref[pl.ds(r, rows), :] = w_ref[pl.ds(r, rows), :].astype(BF16)
        return carry

    lax.fori_loop(0, k // rows, body, 0)


def _mm_kernel(a_ref, *refs, n_w, epilogue):
    w_refs = refs[:n_w]
    pos = n_w
    res_ref = None
    if epilogue == "residual":
        res_ref = refs[pos]
        pos += 1
    o_ref = refs[pos]
    wbf_refs = refs[pos + 1:pos + 1 + n_w]

    @pl.when(pl.program_id(1) == 0)
    def _():
        for w_ref, wbf_ref in zip(w_refs, wbf_refs):
            _cast_weight(w_ref, wbf_ref)

    a = a_ref[...].astype(BF16)
    acc = jnp.dot(a, wbf_refs[0][...], preferred_element_type=F32)
    if epilogue == "swiglu":
        up = jnp.dot(a, wbf_refs[1][...], preferred_element_type=F32)
        acc = acc * jax.nn.sigmoid(acc) * up
    elif epilogue == "residual":
        acc = acc + res_ref[...]
    o_ref[...] = acc.astype(o_ref.dtype)


def ws_matmul(a, ws, *, tm, tn, epilogue, out_dtype, res=None, name):
    m, k = a.shape
    n = ws[0].shape[-1]
    n_w = len(ws)
    in_specs = [pl.BlockSpec((tm, k), lambda j, i: (i, 0))]
    in_specs += [pl.BlockSpec((None, k, tn), lambda j, i: (0, 0, j)) for _ in ws]
    operands = [a, *ws]
    if epilogue == "residual":
        in_specs.append(pl.BlockSpec((tm, tn), lambda j, i: (i, j)))
        operands.append(res)
    return pl.pallas_call(
        functools.partial(_mm_kernel, n_w=n_w, epilogue=epilogue),
        grid=(n // tn, m // tm),
        in_specs=in_specs,
        out_specs=pl.BlockSpec((tm, tn), lambda j, i: (i, j)),
        out_shape=jax.ShapeDtypeStruct((m, n), out_dtype),
        scratch_shapes=[pltpu.VMEM((k, tn), BF16) for _ in ws],
        compiler_params=_cparams(("arbitrary", "arbitrary")),
        name=name,
    )(*operands)


def _silu_mul(g, u):
    return g * jax.nn.sigmoid(g) * u


def _fused_ffn_kernel(ce_ref, cs_ref, cn_ref, used_ref, x_hbm, wg_ref, wu_ref, wd_ref, y_hbm,
                      wg_bf, wu_bf, wd_bf, xbuf, acc, hbuf, stage, pend, x_sem, y_sem,
                      *, tm, n_tiles):
    del ce_ref
    c = pl.program_id(0)
    f = pl.program_id(1)
    last_c = pl.num_programs(0) - 1
    last_f = pl.num_programs(1) - 1
    n = cn_ref[c]
    t0 = cs_ref[c]

    def x_copy(t, slot):
        row = pl.multiple_of((t0 + t) * tm, tm)
        return pltpu.make_async_copy(x_hbm.at[pl.ds(row, tm), :], stage.at[slot], x_sem.at[slot])

    def y_copy(tile, t):
        row = pl.multiple_of(tile * tm, tm)
        return pltpu.make_async_copy(acc.at[t], y_hbm.at[pl.ds(row, tm), :], y_sem)

    def drain_y():
        def wait(t, carry):
            y_copy(0, 0).wait()
            return carry

        lax.fori_loop(0, pend[0], wait, 0)
        pend[0] = 0

    @pl.when((c == 0) & (f == 0))
    def _():
        pend[0] = 0

    @pl.when((n > 0) & (f == 0))
    def _():
        x_copy(0, 0).start()
        drain_y()

        def load(t, carry):
            slot = lax.rem(t, 2)

            @pl.when(t + 1 < n)
            def _():
                x_copy(t + 1, 1 - slot).start()

            x_copy(t, slot).wait()
            xbuf[t] = stage[slot].astype(BF16)
            acc[t] = jnp.zeros(acc.shape[1:], F32)
            return carry

        lax.fori_loop(0, n, load, 0)

    @pl.when(n > 0)
    def _():
        for w_ref, wbf_ref in ((wg_ref, wg_bf), (wu_ref, wu_bf), (wd_ref, wd_bf)):
            _cast_weight(w_ref, wbf_ref)

        def up(t):
            a = xbuf[t]
            g = jnp.dot(a, wg_bf[...], preferred_element_type=F32)
            u = jnp.dot(a, wu_bf[...], preferred_element_type=F32)
            return _silu_mul(g, u).astype(BF16)

        def down(t, h):
            acc[t] = acc[t] + jnp.dot(h, wd_bf[...], preferred_element_type=F32)

        hbuf[0] = up(0)

        def body(t, carry):
            h_prev = hbuf[lax.rem(t - 1, 2)]
            h = up(t)
            down(t - 1, h_prev)
            hbuf[lax.rem(t, 2)] = h
            return carry

        lax.fori_loop(1, n, body, 0)
        down(n - 1, hbuf[lax.rem(n - 1, 2)])

    @pl.when((n > 0) & (f == last_f))
    def _():
        def store(t, carry):
            y_copy(t0 + t, t).start()
            return carry

        lax.fori_loop(0, n, store, 0)
        pend[0] = n

    @pl.when((c == last_c) & (f == last_f))
    def _():
        drain_y()
        acc[0] = jnp.zeros(acc.shape[1:], F32)

        def fill(tile, carry):
            y_copy(tile, 0).start()
            y_copy(tile, 0).wait()
            return carry

        lax.fori_loop(used_ref[0], n_tiles, fill, 0)


def fused_expert_swiglu(x, wg, wu, wd, chunks, *, tm, chunk_tiles, tf, name):
    m, k = x.shape
    _, _, d_ff = wg.shape
    d_out = wd.shape[-1]
    n_chunks = chunks[0].shape[0]
    n_f = d_ff // tf

    def col(c, f, cn):
        return jnp.where(cn[c] > 0, f, n_f - 1)

    grid_spec = pltpu.PrefetchScalarGridSpec(
        num_scalar_prefetch=4,
        grid=(n_chunks, n_f),
        in_specs=[
            pl.BlockSpec(memory_space=pl.ANY),
            pl.BlockSpec((None, k, tf), lambda c, f, ce, cs, cn, us: (ce[c], 0, col(c, f, cn))),
            pl.BlockSpec((None, k, tf), lambda c, f, ce, cs, cn, us: (ce[c], 0, col(c, f, cn))),
            pl.BlockSpec((None, tf, d_out), lambda c, f, ce, cs, cn, us: (ce[c], col(c, f, cn), 0)),
        ],
        out_specs=pl.BlockSpec(memory_space=pl.ANY),
        scratch_shapes=[
            pltpu.VMEM((k, tf), BF16), pltpu.VMEM((k, tf), BF16), pltpu.VMEM((tf, d_out), BF16),
            pltpu.VMEM((chunk_tiles, tm, k), BF16),
            pltpu.VMEM((chunk_tiles, tm, d_out), F32),
            pltpu.VMEM((2, tm, tf), BF16),
            pltpu.VMEM((2, tm, k), x.dtype),
            pltpu.SMEM((1,), jnp.int32),
            pltpu.SemaphoreType.DMA((2,)),
            pltpu.SemaphoreType.DMA(()),
        ],
    )
    return pl.pallas_call(
        functools.partial(_fused_ffn_kernel, tm=tm, n_tiles=m // tm),
        grid_spec=grid_spec,
        out_shape=jax.ShapeDtypeStruct((m, d_out), F32),
        compiler_params=_cparams(("arbitrary", "arbitrary")),
        name=name,
    )(*chunks, x, wg, wu, wd)


def _na_bias_tiles(rpb, rows):
    n_heads, _, n_co = rpb.shape
    c = np.arange(GRID_W)
    c_start = np.clip(c - NA_KW // 2, 0, GRID_W - NA_KW)
    col_in = (c[None, :] >= c_start[:, None]) & (c[None, :] < c_start[:, None] + NA_KW)
    col_off = np.clip(c[None, :] - c[:, None], -(NA_KW - 1), NA_KW - 1) + NA_KW - 1
    pick = (col_off[None] == np.arange(n_co)[:, None, None]).astype(np.float32)
    blocks = jnp.einsum("hrj,jqk->hrqk", rpb.astype(F32) * LOG2E, pick,
                        precision=lax.Precision.HIGHEST)
    blocks = jnp.where(col_in[None, None], blocks, NEG_INF)
    masked = jnp.full((n_heads, GRID_W, GRID_W), NEG_INF, F32)
    tiles = []
    for r0, ks in ((0, 0), (NA_Q_ROWS, NA_Q_ROWS - NA_KH // 2), (rows - NA_Q_ROWS, rows - NA_K_ROWS)):
        q_rows = []
        for qrow in range(r0, r0 + NA_Q_ROWS):
            r_start = min(max(qrow - NA_KH // 2, 0), rows - NA_KH)
            q_rows.append(jnp.concatenate(
                [blocks[:, krow - qrow + NA_KH - 1] if r_start <= krow < r_start + NA_KH else masked
                 for krow in range(ks, ks + NA_K_ROWS)], axis=-1))
        tiles.append(jnp.concatenate(q_rows, axis=1))
    return jnp.stack(tiles, axis=1)


def _na_kernel(q_ref, k_ref, v_ref, b_ref, o_ref, *, rows):
    rb = pl.program_id(2)
    ks = jnp.clip(rb * NA_Q_ROWS - NA_KH // 2, 0, rows - NA_K_ROWS)
    k0 = pl.multiple_of(ks * GRID_W, GRID_W)
    nk = NA_K_ROWS * GRID_W
    kb = k_ref[pl.ds(k0, nk), :]
    vb = v_ref[pl.ds(k0, nk), :]
    s = lax.dot_general(q_ref[...], kb, (((1,), (1,)), ((), ())), preferred_element_type=F32)
    s = s * (HEAD_DIM ** -0.5 * LOG2E) + b_ref[...]
    p = jnp.exp2(s - jnp.max(s, axis=-1, keepdims=True)).astype(BF16)
    v1 = jnp.concatenate([vb, jnp.ones_like(vb)], axis=-1)
    o = jnp.dot(p, v1, preferred_element_type=F32)
    o_ref[...] = (o[:, :HEAD_DIM] / o[:, HEAD_DIM:]).astype(o_ref.dtype)


def neighbourhood_attention(proj, rpb):
    b, s, _ = proj.shape
    rows = s // GRID_W
    n_rb = rows // NA_Q_ROWS
    tq = NA_Q_ROWS * GRID_W
    bias = _na_bias_tiles(rpb, rows)

    def pat(r):
        return jnp.where(r == 0, 0, jnp.where(r == n_rb - 1, 2, 1))

    return pl.pallas_call(
        functools.partial(_na_kernel, rows=rows),
        grid=(b, NA_HEADS, n_rb),
        in_specs=[
            pl.BlockSpec((None, tq, HEAD_DIM), lambda bi, h, r: (bi, r, h)),
            pl.BlockSpec((None, s, HEAD_DIM), lambda bi, h, r: (bi, 0, NA_HEADS + h)),
            pl.BlockSpec((None, s, HEAD_DIM), lambda bi, h, r: (bi, 0, 2 * NA_HEADS + h)),
            pl.BlockSpec((None, None, tq, NA_K_ROWS * GRID_W), lambda bi, h, r: (h, pat(r), 0, 0)),
        ],
        out_specs=pl.BlockSpec((None, tq, HEAD_DIM), lambda bi, h, r: (bi, r, h)),
        out_shape=jax.ShapeDtypeStruct((b, s, NA_HEADS * HEAD_DIM), BF16),
        compiler_params=_cparams(("arbitrary", "arbitrary", "arbitrary")),
        name="na_attention",
    )(proj, proj, proj, bias)


def _rope_tables(seq):
    dq = HEAD_DIM // 2
    inv = 1.0 / (ROPE_THETA ** (jnp.arange(0, dq, 2, dtype=F32) / dq))
    ang = jnp.arange(seq, dtype=F32)[:, None] * inv[None, :]
    cos, sin = jnp.cos(ang), jnp.sin(ang)
    zero = jnp.zeros_like(sin)
    c = jnp.concatenate([cos, cos, cos, cos], axis=-1)
    s_first = jnp.concatenate([-sin, zero, -sin, zero], axis=-1)
    s_second = jnp.concatenate([zero, sin, zero, sin], axis=-1)
    return c, s_first, s_second


def _rope(x, c, s_first, s_second):
    half = HEAD_DIM // 4
    return (x * c + pltpu.roll(x, HEAD_DIM - half, axis=1) * s_first
            + pltpu.roll(x, half, axis=1) * s_second)


def _da_kernel(lam_ref, q_ref, k_ref, v_ref, cq_ref, s1q_ref, s2q_ref, ck_ref, s1k_ref, s2k_ref,
               g_ref, o_ref, kr_ref, *, out_scale):
    @pl.when(pl.program_id(2) == 0)
    def _():
        k = k_ref[...].astype(F32)
        kr_ref[...] = _rope(k, ck_ref[...], s1k_ref[...], s2k_ref[...]).astype(BF16)

    dq = HEAD_DIM // 2
    q = _rope(q_ref[...].astype(F32), cq_ref[...], s1q_ref[...], s2q_ref[...]) * (dq ** -0.5 * LOG2E)
    lane = lax.broadcasted_iota(jnp.int32, q.shape, 1)
    q1 = jnp.where(lane < dq, q, 0.0).astype(BF16)
    q2 = jnp.where(lane >= dq, q, 0.0).astype(BF16)
    kr = kr_ref[...]
    dn = (((1,), (1,)), ((), ()))
    s1 = lax.dot_general(q1, kr, dn, preferred_element_type=F32)
    s2 = lax.dot_general(q2, kr, dn, preferred_element_type=F32)
    p1 = jnp.exp2(s1 - jnp.max(s1, axis=-1, keepdims=True))
    p2 = jnp.exp2(s2 - jnp.max(s2, axis=-1, keepdims=True))
    w1 = 1.0 / jnp.sum(p1, axis=-1, keepdims=True)
    w2 = lam_ref[0] / jnp.sum(p2, axis=-1, keepdims=True)
    attn = (p1 * w1 - p2 * w2).astype(BF16)
    o = jnp.dot(attn, v_ref[...], preferred_element_type=F32)
    o_ref[...] = (_rms(o, g_ref[...]) * out_scale).astype(o_ref.dtype)


def differential_attention(proj, lam, subln, lambda_init):
    b, s, _ = proj.shape
    tq = min(DA_Q_BLOCK, s)
    base = 3 * NA_HEADS
    c, s_first, s_second = _rope_tables(s)
    qtab = pl.BlockSpec((tq, HEAD_DIM), lambda bi, h, i: (i, 0))
    ktab = pl.BlockSpec((s, HEAD_DIM), lambda bi, h, i: (0, 0))
    return pl.pallas_call(
        functools.partial(_da_kernel, out_scale=1.0 - lambda_init),
        grid=(b, DA_HEADS, s // tq),
        in_specs=[
            pl.BlockSpec(memory_space=pltpu.SMEM),
            pl.BlockSpec((None, tq, HEAD_DIM), lambda bi, h, i: (bi, i, base + h)),
            pl.BlockSpec((None, s, HEAD_DIM), lambda bi, h, i: (bi, 0, base + DA_HEADS + h)),
            pl.BlockSpec((None, s, HEAD_DIM), lambda bi, h, i: (bi, 0, base + 2 * DA_HEADS + h)),
            qtab, qtab, qtab, ktab, ktab, ktab,
            pl.BlockSpec((1, HEAD_DIM), lambda bi, h, i: (0, 0)),
        ],
        out_specs=pl.BlockSpec((None, tq, HEAD_DIM), lambda bi, h, i: (bi, i, h)),
        out_shape=jax.ShapeDtypeStruct((b, s, DA_HEADS * HEAD_DIM), BF16),
        scratch_shapes=[pltpu.VMEM((s, HEAD_DIM), BF16)],
        compiler_params=_cparams(("arbitrary", "arbitrary", "arbitrary")),
        name="da_attention",
    )(lam, proj, proj, proj, c, s_first, s_second, c, s_first, s_second, subln.reshape(1, HEAD_DIM))


SCAN_CHUNK = 512


def _shift_rows(x, k):
    return pltpu.roll(x, k % x.shape[0], axis=0)


def _block_scan(a, b, reverse):
    row = lax.broadcasted_iota(jnp.int32, a.shape, 0) % SUBLANES
    k = 1
    while k < SUBLANES:
        if reverse:
            keep = row < SUBLANES - k
            a_n, b_n = _shift_rows(a, -k), _shift_rows(b, -k)
        else:
            keep = row >= k
            a_n, b_n = _shift_rows(a, k), _shift_rows(b, k)
        b = jnp.where(keep, a * b_n + b, b)
        a = jnp.where(keep, a * a_n, a)
        k *= 2
    return a, b


def _lru_kernel(gate_ref, rec_ref, cw_ref, cb_ref, wa_ref, ba_ref, wx_ref, bx_ref, ap_ref, o_ref,
                u_ref, a_ref, b_ref, h_ref):
    s = rec_ref.shape[0]
    n_chunks = s // SCAN_CHUNK

    r = rec_ref[...].astype(F32)
    t_idx = lax.broadcasted_iota(jnp.int32, r.shape, 0)
    cw = cw_ref[...]
    u = r * cw[1:2, :] + cb_ref[...]
    u = u + jnp.where(t_idx >= 1, _shift_rows(r, 1), 0.0) * cw[0:1, :]
    u = u + jnp.where(t_idx < s - 1, _shift_rows(r, -1), 0.0) * cw[2:3, :]
    u = u + jnp.where(t_idx < s - 2, _shift_rows(r, -2), 0.0) * cw[3:4, :]
    u_ref[...] = u

    for d in range(2):
        reverse = d == 1
        ap = ap_ref[d]
        softplus = jnp.maximum(-ap, 0.0) + jnp.log1p(jnp.exp(-jnp.abs(ap)))
        wa = wa_ref[d].astype(BF16)
        wx = wx_ref[d].astype(BF16)

        def prep(c, carry, d=d, reverse=reverse, softplus=softplus, wa=wa, wx=wx):
            r0 = pl.multiple_of(c * SCAN_CHUNK, SCAN_CHUNK)
            uc = u_ref[pl.ds(r0, SCAN_CHUNK), :]
            ub = uc.astype(BF16)
            ga = jax.nn.sigmoid(jnp.dot(ub, wa, preferred_element_type=F32) + ba_ref[d])
            gx = jax.nn.sigmoid(jnp.dot(ub, wx, preferred_element_type=F32) + bx_ref[d])
            log_a = -LRU_C * ga * softplus
            a = jnp.exp(log_a)
            mult = jnp.sqrt(1.0 - a * a)
            a_s, b_s = _block_scan(a, mult * (gx * uc), reverse)
            a_ref[pl.ds(r0, SCAN_CHUNK), :] = a_s
            b_ref[pl.ds(r0, SCAN_CHUNK), :] = b_s
            return carry

        lax.fori_loop(0, n_chunks, prep, 0)

        n_groups = s // SUBLANES

        def step(i, h_prev, reverse=reverse):
            g = (n_groups - 1 - i) if reverse else i
            r0 = pl.multiple_of(g * SUBLANES, SUBLANES)
            h = a_ref[pl.ds(r0, SUBLANES), :] * h_prev + b_ref[pl.ds(r0, SUBLANES), :]
            if reverse:
                h_ref[pl.ds(r0, SUBLANES), :] = h_ref[pl.ds(r0, SUBLANES), :] + h
            else:
                h_ref[pl.ds(r0, SUBLANES), :] = h
            edge = h[0:1, :] if reverse else h[SUBLANES - 1:SUBLANES, :]
            return jnp.broadcast_to(edge, h.shape)

        lax.fori_loop(0, n_groups, step, jnp.zeros((SUBLANES, LRU_BLOCK_DIM), F32), unroll=8)

    y = jax.nn.gelu(gate_ref[...].astype(F32), approximate=True)
    o_ref[...] = (y * h_ref[...]).astype(o_ref.dtype)


def recurrent_block(proj, conv_w, conv_b, w_a, b_a, w_x, b_x, a_param):
    b, s, w2 = proj.shape
    w = w2 // 2
    nb = w // LRU_BLOCK_DIM
    c = LRU_BLOCK_DIM
    vec = lambda p: p.reshape(2, 1, w)
    return pl.pallas_call(
        _lru_kernel,
        grid=(b, nb),
        in_specs=[
            pl.BlockSpec((None, s, c), lambda bi, n: (bi, 0, n)),
            pl.BlockSpec((None, s, c), lambda bi, n: (bi, 0, nb + n)),
            pl.BlockSpec((conv_w.shape[0], c), lambda bi, n: (0, n)),
            pl.BlockSpec((1, c), lambda bi, n: (0, n)),
            pl.BlockSpec((2, None, c, c), lambda bi, n: (0, n, 0, 0)),
            pl.BlockSpec((2, 1, c), lambda bi, n: (0, 0, n)),
            pl.BlockSpec((2, None, c, c), lambda bi, n: (0, n, 0, 0)),
            pl.BlockSpec((2, 1, c), lambda bi, n: (0, 0, n)),
            pl.BlockSpec((2, 1, c), lambda bi, n: (0, 0, n)),
        ],
        out_specs=pl.BlockSpec((None, s, c), lambda bi, n: (bi, 0, n)),
        out_shape=jax.ShapeDtypeStruct((b, s, w), BF16),
        scratch_shapes=[pltpu.VMEM((s, c), F32) for _ in range(4)],
        compiler_params=_cparams(("arbitrary", "arbitrary")),
        name="rg_lru",
    )(proj, proj, conv_w, conv_b.reshape(1, w), w_a, vec(b_a), w_x, vec(b_x), vec(a_param))


R_E0, R_E1, R_G0, R_G1, R_RANK0, R_RANK1 = range(6)


def _route_kernel(x_ref, g_ref, r_ref, meta_ref, cnt_ref, carry_ref):
    i = pl.program_id(0)

    @pl.when(i == 0)
    def _():
        carry_ref[...] = jnp.zeros_like(carry_ref)

    h = _rms(x_ref[...], g_ref[...])
    logits = jnp.dot(h, r_ref[...], preferred_element_type=F32, precision=lax.Precision.HIGHEST)
    tm = logits.shape[0]
    lane = lax.broadcasted_iota(jnp.int32, logits.shape, 1).astype(F32)
    logits = jnp.where(lane < N_EXPERTS, logits, -jnp.inf)
    m0 = jnp.max(logits, axis=-1, keepdims=True)
    e0 = jnp.min(jnp.where(logits == m0, lane, float(LANES)), axis=-1, keepdims=True)
    rest = jnp.where(lane == e0, -jnp.inf, logits)
    m1 = jnp.max(rest, axis=-1, keepdims=True)
    e1 = jnp.min(jnp.where(rest == m1, lane, float(LANES)), axis=-1, keepdims=True)
    z = jnp.exp(m1 - m0)
    g0 = 1.0 / (1.0 + z)
    g1 = z / (1.0 + z)

    hit0 = lane == e0
    hit1 = lane == e1
    both = jnp.where(hit0 | hit1, 1.0, 0.0).astype(BF16)
    ri = lax.broadcasted_iota(jnp.int32, (tm, tm), 0)
    ci = lax.broadcasted_iota(jnp.int32, (tm, tm), 1)
    lower = jnp.where(ci < ri, 1.0, 0.0).astype(BF16)
    before = jnp.dot(lower, both, preferred_element_type=F32) + carry_ref[0:1, :]
    rank0 = jnp.sum(jnp.where(hit0, before, 0.0), axis=-1, keepdims=True)
    rank1 = jnp.sum(jnp.where(hit1, before, 0.0), axis=-1, keepdims=True)
    total = carry_ref[0:1, :] + jnp.sum(both.astype(F32), axis=0, keepdims=True)
    carry_ref[...] = jnp.broadcast_to(total, carry_ref.shape)
    cnt_ref[...] = jnp.broadcast_to(total, cnt_ref.shape)

    rec = jnp.zeros(logits.shape, F32)
    for col, val in ((R_E0, e0), (R_E1, e1), (R_G0, g0), (R_G1, g1),
                     (R_RANK0, rank0), (R_RANK1, rank1)):
        rec = jnp.where(lane == col, val, rec)
    meta_ref[...] = rec


def route(x, g, router):
    t, d = x.shape
    tm = ROUTE_TILE
    r_pad = jnp.zeros((d, LANES), F32).at[:, :N_EXPERTS].set(router)
    return pl.pallas_call(
        _route_kernel,
        grid=(t // tm,),
        in_specs=[pl.BlockSpec((tm, d), lambda i: (i, 0)),
                  pl.BlockSpec((1, d), lambda i: (0, 0)),
                  pl.BlockSpec((d, LANES), lambda i: (0, 0))],
        out_specs=[pl.BlockSpec((tm, LANES), lambda i: (i, 0)),
                   pl.BlockSpec((SUBLANES, LANES), lambda i: (0, 0))],
        out_shape=[jax.ShapeDtypeStruct((t, LANES), F32),
                   jax.ShapeDtypeStruct((SUBLANES, LANES), F32)],
        scratch_shapes=[pltpu.VMEM((SUBLANES, LANES), F32)],
        compiler_params=_cparams(("arbitrary",)),
        name="moe_route",
    )(x, g.reshape(1, d), r_pad)


def _row_copy(src_ref, src_row, dst_ref, dst_row, sem):
    return pltpu.make_async_copy(src_ref.at[pl.ds(src_row, 1), :], dst_ref.at[pl.ds(dst_row, 1), :], sem)


def _dispatch_kernel(p0_ref, p1_ref, x_ref, g_ref, init_ref, xs_ref, h_ref, sem):
    del init_ref
    tm = x_ref.shape[0]
    base = pl.program_id(0) * tm
    h_ref[...] = _rms(x_ref[...], g_ref[...])

    def start(t, carry):
        _row_copy(h_ref, t, xs_ref, p0_ref[base + t], sem).start()
        _row_copy(h_ref, t, xs_ref, p1_ref[base + t], sem).start()
        return carry

    lax.fori_loop(0, tm, start, 0)

    def wait(t, carry):
        _row_copy(h_ref, 0, xs_ref, 0, sem).wait()
        _row_copy(h_ref, 0, xs_ref, 0, sem).wait()
        return carry

    lax.fori_loop(0, tm, wait, 0)


def dispatch(x, g, pos0, pos1, n_rows):
    t, d = x.shape
    tm = ROUTE_TILE
    grid_spec = pltpu.PrefetchScalarGridSpec(
        num_scalar_prefetch=2,
        grid=(t // tm,),
        in_specs=[pl.BlockSpec((tm, d), lambda i, p0, p1: (i, 0)),
                  pl.BlockSpec((1, d), lambda i, p0, p1: (0, 0)),
                  pl.BlockSpec(memory_space=pl.ANY)],
        out_specs=pl.BlockSpec(memory_space=pl.ANY),
        scratch_shapes=[pltpu.VMEM((tm, d), F32), pltpu.SemaphoreType.DMA(())],
    )
    return pl.pallas_call(
        _dispatch_kernel,
        grid_spec=grid_spec,
        out_shape=jax.ShapeDtypeStruct((n_rows, d), F32),
        input_output_aliases={4: 0},
        compiler_params=_cparams(("arbitrary",)),
        name="moe_dispatch",
    )(pos0, pos1, x, g.reshape(1, d), jnp.zeros((n_rows, d), F32))


def _combine_kernel(p0_ref, p1_ref, x_ref, meta_ref, g_ref, y_ref, o_ref, y0_ref, y1_ref, sem):
    tm = x_ref.shape[0]
    base = pl.program_id(0) * tm

    def start(t, carry):
        _row_copy(y_ref, p0_ref[base + t], y0_ref, t, sem).start()
        _row_copy(y_ref, p1_ref[base + t], y1_ref, t, sem).start()
        return carry

    lax.fori_loop(0, tm, start, 0)

    def wait(t, carry):
        _row_copy(y_ref, 0, y0_ref, 0, sem).wait()
        _row_copy(y_ref, 0, y1_ref, 0, sem).wait()
        return carry

    lax.fori_loop(0, tm, wait, 0)
    meta = meta_ref[...]
    g0 = meta[:, R_G0:R_G0 + 1]
    g1 = meta[:, R_G1:R_G1 + 1]
    out = x_ref[...] + g0 * y0_ref[...] + g1 * y1_ref[...]
    o_ref[...] = _rms(out, g_ref[...])


def combine(x, meta, g, y, pos0, pos1):
    t, d = x.shape
    tm = ROUTE_TILE
    grid_spec = pltpu.PrefetchScalarGridSpec(
        num_scalar_prefetch=2,
        grid=(t // tm,),
        in_specs=[pl.BlockSpec((tm, d), lambda i, p0, p1: (i, 0)),
                  pl.BlockSpec((tm, LANES), lambda i, p0, p1: (i, 0)),
                  pl.BlockSpec((1, d), lambda i, p0, p1: (0, 0)),
                  pl.BlockSpec(memory_space=pl.ANY)],
        out_specs=pl.BlockSpec((tm, d), lambda i, p0, p1: (i, 0)),
        scratch_shapes=[pltpu.VMEM((tm, d), F32), pltpu.VMEM((tm, d), F32),
                        pltpu.SemaphoreType.DMA(())],
    )
    return pl.pallas_call(
        _combine_kernel,
        grid_spec=grid_spec,
        out_shape=jax.ShapeDtypeStruct((t, d), F32),
        compiler_params=_cparams(("arbitrary",)),
        name="moe_combine",
    )(pos0, pos1, x, meta, g.reshape(1, d), y)


def _moe_plan(meta, counts, tm):
    counts = counts[0, :N_EXPERTS].astype(jnp.int32)
    group_tiles = (counts + tm - 1) // tm
    group_start = jnp.cumsum(group_tiles) - group_tiles
    e0 = meta[:, R_E0].astype(jnp.int32)
    e1 = meta[:, R_E1].astype(jnp.int32)
    pos0 = group_start[e0] * tm + meta[:, R_RANK0].astype(jnp.int32)
    pos1 = group_start[e1] * tm + meta[:, R_RANK1].astype(jnp.int32)
    return pos0, pos1, group_start, group_tiles


def _moe_chunks(group_start, group_tiles, chunk_tiles, n_chunks):
    per = (group_tiles + chunk_tiles - 1) // chunk_tiles
    c_end = jnp.cumsum(per)
    c_start = c_end - per
    total = c_end[-1]
    c = jnp.arange(n_chunks, dtype=jnp.int32)
    c_eff = jnp.minimum(c, jnp.maximum(total - 1, 0))
    expert = jnp.sum((c_eff[:, None] >= c_end[None, :]).astype(jnp.int32), axis=1)
    expert = jnp.minimum(expert, N_EXPERTS - 1)
    k = c_eff - c_start[expert]
    first_tile = group_start[expert] + k * chunk_tiles
    tiles = jnp.clip(group_tiles[expert] - k * chunk_tiles, 0, chunk_tiles)
    tiles = jnp.where(c < total, tiles, 0)
    used = (group_start[-1] + group_tiles[-1]).reshape(1)
    return expert, first_tile, tiles, used


def kernel(x, ev_mix_norm, ev_w_in, ev_na_rpb, ev_da_lambda_q1, ev_da_lambda_k1, ev_da_lambda_q2, ev_da_lambda_k2, ev_da_subln, ev_w_out, ev_ffn_norm, ev_ffn_w_gate, ev_ffn_w_up, ev_ffn_w_down, od_mix_norm, od_w_in, od_conv_w, od_conv_b, od_lru_w_a, od_lru_b_a, od_lru_w_x, od_lru_b_x, od_lru_a_param, od_w_out, od_ffn_norm, od_router, od_moe_w_gate, od_moe_w_up, od_moe_w_down, final_norm):
    b, s, d = x.shape
    t = b * s
    xt = x.reshape(t, d)

    lambda_init = 0.8 - 0.6 * math.exp(-0.3 * 0)
    h = rmsnorm(xt, ev_mix_norm[0], BF16)
    proj = ws_matmul(h, [ev_w_in], tm=512, tn=1024, epilogue="cast", out_dtype=BF16, name="l0_in_proj")
    proj = proj.reshape(b, s, -1)
    na_out = neighbourhood_attention(proj, ev_na_rpb[0])
    lam = (jnp.exp(jnp.sum(ev_da_lambda_q1[0] * ev_da_lambda_k1[0]))
           - jnp.exp(jnp.sum(ev_da_lambda_q2[0] * ev_da_lambda_k2[0])) + lambda_init).reshape(1)
    da_out = differential_attention(proj, lam, ev_da_subln[0], lambda_init)
    mix = jnp.concatenate([na_out, da_out], axis=-1).reshape(t, -1)
    xt = ws_matmul(mix, [ev_w_out], tm=512, tn=1024, epilogue="residual", out_dtype=F32, res=xt,
                   name="l0_out_proj")

    h = rmsnorm(xt, ev_ffn_norm[0], BF16)
    act = ws_matmul(h, [ev_ffn_w_gate, ev_ffn_w_up], tm=512, tn=1024, epilogue="swiglu",
                    out_dtype=BF16, name="l0_ffn_up")
    xt = ws_matmul(act, [ev_ffn_w_down], tm=256, tn=512, epilogue="residual", out_dtype=F32, res=xt,
                   name="l0_ffn_down")

    h = rmsnorm(xt, od_mix_norm[0], BF16)
    proj = ws_matmul(h, [od_w_in], tm=512, tn=1024, epilogue="cast", out_dtype=BF16, name="l1_in_proj")
    rec = recurrent_block(proj.reshape(b, s, -1), od_conv_w[0], od_conv_b[0], od_lru_w_a[0],
                          od_lru_b_a[0], od_lru_w_x[0], od_lru_b_x[0], od_lru_a_param[0])
    xt = ws_matmul(rec.reshape(t, -1), [od_w_out], tm=512, tn=1024, epilogue="residual",
                   out_dtype=F32, res=xt, name="l1_out_proj")

    tm = MOE_ROW_TILE
    n_tiles = (t * 2) // tm + N_EXPERTS
    meta, counts = route(xt, od_ffn_norm[0], od_router[0])
    pos0, pos1, group_start, group_tiles = _moe_plan(meta, counts, tm)
    xs = dispatch(xt, od_ffn_norm[0], pos0, pos1, n_tiles * tm)
    n_chunks = (n_tiles - N_EXPERTS) // MOE_CHUNK_TILES + N_EXPERTS
    chunks = _moe_chunks(group_start, group_tiles, MOE_CHUNK_TILES, n_chunks)
    y = fused_expert_swiglu(xs, od_moe_w_gate[0], od_moe_w_up[0], od_moe_w_down[0], chunks, tm=tm,
                            chunk_tiles=MOE_CHUNK_TILES, tf=FFN_COL_TILE, name="moe_ffn")
    out = combine(xt, meta, final_norm, y, pos0, pos1)
    return out.reshape(b, s, d)
```

```python
import functools
import math

import numpy as np
import jax
import jax.numpy as jnp
from jax import lax
from jax.experimental import pallas as pl
from jax.experimental.pallas import tpu as pltpu

F32 = jnp.float32
BF16 = jnp.bfloat16

HEAD_DIM = 128
NA_HEADS = 8
DA_HEADS = 8
GRID_W = 64
NA_KH = 8
NA_KW = 16
ROPE_THETA = 10000.0
LRU_BLOCK_DIM = 128
LRU_C = 8.0
N_EXPERTS = 8
EPS = 1e-6
NEG_INF = -1e30
LOG2E = math.log2(math.e)

NA_Q_ROWS = 8
NA_K_ROWS = NA_Q_ROWS + NA_KH
DA_Q_BLOCK = 512
DA_KEY_CHUNK = 256
MOE_ROW_TILE = 256
MOE_CHUNK_TILES = 9
FFN_COL_TILE = 256
ROUTE_TILE = 256
ROW_COPY_UNROLL = 8
LANES = 128
SUBLANES = 8
VMEM_LIMIT = 56 * 1024 * 1024


def _cparams(semantics):
    return pltpu.CompilerParams(dimension_semantics=semantics, vmem_limit_bytes=VMEM_LIMIT)


def _rms(x, g):
    return x * lax.rsqrt(jnp.mean(x * x, axis=-1, keepdims=True) + EPS) * g


def _rmsnorm_kernel(x_ref, g_ref, o_ref):
    o_ref[...] = _rms(x_ref[...], g_ref[...]).astype(o_ref.dtype)


def rmsnorm(x, g, out_dtype, tm=512):
    t, d = x.shape
    return pl.pallas_call(
        _rmsnorm_kernel,
        grid=(t // tm,),
        in_specs=[pl.BlockSpec((tm, d), lambda i: (i, 0)),
                  pl.BlockSpec((1, d), lambda i: (0, 0))],
        out_specs=pl.BlockSpec((tm, d), lambda i: (i, 0)),
        out_shape=jax.ShapeDtypeStruct((t, d), out_dtype),
        compiler_params=_cparams(("arbitrary",)),
        name="rmsnorm",
    )(x, g.reshape(1, d))


CAST_ROWS = 256


def _cast_weight(w_ref, wbf_ref):
    k = w_ref.shape[0]
    rows = min(CAST_ROWS, k)
    assert k % rows == 0

    def body(c, carry):
        r = pl.multiple_of(c * rows, rows)
        wbf_ref[pl.ds(r, rows), :] = w_ref[pl.ds(r, rows), :].astype(BF16)
        return carry

    lax.fori_loop(0, k // rows, body, 0)


def _mm_kernel(a_ref, *refs, n_w, epilogue):
    w_refs = refs[:n_w]
    pos = n_w
    res_ref = None
    if epilogue == "residual":
        res_ref = refs[pos]
        pos += 1
    o_ref = refs[pos]
    wbf_refs = refs[pos + 1:pos + 1 + n_w]

    @pl.when(pl.program_id(1) == 0)
    def _():
        for w_ref, wbf_ref in zip(w_refs, wbf_refs):
            _cast_weight(w_ref, wbf_ref)

    a = a_ref[...].astype(BF16)
    acc = jnp.dot(a, wbf_refs[0][...], preferred_element_type=F32)
    if epilogue == "swiglu":
        up = jnp.dot(a, wbf_refs[1][...], preferred_element_type=F32)
        acc = acc * jax.nn.sigmoid(acc) * up
    elif epilogue == "residual":
        acc = acc + res_ref[...]
    o_ref[...] = acc.astype(o_ref.dtype)


def ws_matmul(a, ws, *, tm, tn, epilogue, out_dtype, res=None, name):
    m, k = a.shape
    n = ws[0].shape[-1]
    n_w = len(ws)
    in_specs = [pl.BlockSpec((tm, k), lambda j, i: (i, 0))]
    in_specs += [pl.BlockSpec((None, k, tn), lambda j, i: (0, 0, j)) for _ in ws]
    operands = [a, *ws]
    if epilogue == "residual":
        in_specs.append(pl.BlockSpec((tm, tn), lambda j, i: (i, j)))
        operands.append(res)
    return pl.pallas_call(
        functools.partial(_mm_kernel, n_w=n_w, epilogue=epilogue),
        grid=(n // tn, m // tm),
        in_specs=in_specs,
        out_specs=pl.BlockSpec((tm, tn), lambda j, i: (i, j)),
        out_shape=jax.ShapeDtypeStruct((m, n), out_dtype),
        scratch_shapes=[pltpu.VMEM((k, tn), BF16) for _ in ws],
        compiler_params=_cparams(("arbitrary", "arbitrary")),
        name=name,
    )(*operands)


def _silu_mul(g, u):
    return g * jax.nn.sigmoid(g) * u


def _fused_ffn_kernel(ce_ref, cs_ref, cn_ref, used_ref, x_hbm, wg_ref, wu_ref, wd_ref, y_hbm,
                      wg_bf, wu_bf, wd_bf, xbuf, acc, hbuf, stage, pend, x_sem, y_sem,
                      *, tm, n_tiles):
    del ce_ref
    c = pl.program_id(0)
    f = pl.program_id(1)
    last_c = pl.num_programs(0) - 1
    last_f = pl.num_programs(1) - 1
    n = cn_ref[c]
    t0 = cs_ref[c]

    def x_copy(t, slot):
        row = pl.multiple_of((t0 + t) * tm, tm)
        return pltpu.make_async_copy(x_hbm.at[pl.ds(row, tm), :], stage.at[slot], x_sem.at[slot])

    def y_copy(tile, t):
        row = pl.multiple_of(tile * tm, tm)
        return pltpu.make_async_copy(acc.at[t], y_hbm.at[pl.ds(row, tm), :], y_sem)

    def drain_y():
        def wait(t, carry):
            y_copy(0, 0).wait()
            return carry

        lax.fori_loop(0, pend[0], wait, 0)
        pend[0] = 0

    @pl.when((c == 0) & (f == 0))
    def _():
        pend[0] = 0

    @pl.when((n > 0) & (f == 0))
    def _():
        x_copy(0, 0).start()
        drain_y()

        def load(t, carry):
            slot = lax.rem(t, 2)

            @pl.when(t + 1 < n)
            def _():
                x_copy(t + 1, 1 - slot).start()

            x_copy(t, slot).wait()
            xbuf[t] = stage[slot].astype(BF16)
            acc[t] = jnp.zeros(acc.shape[1:], F32)
            return carry

        lax.fori_loop(0, n, load, 0)

    @pl.when(n > 0)
    def _():
        for w_ref, wbf_ref in ((wg_ref, wg_bf), (wu_ref, wu_bf), (wd_ref, wd_bf)):
            _cast_weight(w_ref, wbf_ref)

        def up(t):
            a = xbuf[t]
            g = jnp.dot(a, wg_bf[...], preferred_element_type=F32)
            u = jnp.dot(a, wu_bf[...], preferred_element_type=F32)
            return _silu_mul(g, u).astype(BF16)

        def down(t, h):
            acc[t] = acc[t] + jnp.dot(h, wd_bf[...], preferred_element_type=F32)

        hbuf[0] = up(0)

        def body(t, carry):
            h_prev = hbuf[lax.rem(t - 1, 2)]
            h = up(t)
            down(t - 1, h_prev)
            hbuf[lax.rem(t, 2)] = h
            return carry

        lax.fori_loop(1, n, body, 0)
        down(n - 1, hbuf[lax.rem(n - 1, 2)])

    @pl.when((n > 0) & (f == last_f))
    def _():
        def store(t, carry):
            y_copy(t0 + t, t).start()
            return carry

        lax.fori_loop(0, n, store, 0)
        pend[0] = n

    @pl.when((c == last_c) & (f == last_f))
    def _():
        drain_y()
        acc[0] = jnp.zeros(acc.shape[1:], F32)

        def fill(tile, carry):
            y_copy(tile, 0).start()
            y_copy(tile, 0).wait()
            return carry

        lax.fori_loop(used_ref[0], n_tiles, fill, 0)


def fused_expert_swiglu(x, wg, wu, wd, chunks, *, tm, chunk_tiles, tf, name):
    m, k = x.shape
    _, _, d_ff = wg.shape
    d_out = wd.shape[-1]
    n_chunks = chunks[0].shape[0]
    n_f = d_ff // tf

    def col(c, f, cn):
        return jnp.where(cn[c] > 0, f, n_f - 1)

    grid_spec = pltpu.PrefetchScalarGridSpec(
        num_scalar_prefetch=4,
        grid=(n_chunks, n_f),
        in_specs=[
            pl.BlockSpec(memory_space=pl.ANY),
            pl.BlockSpec((None, k, tf), lambda c, f, ce, cs, cn, us: (ce[c], 0, col(c, f, cn))),
            pl.BlockSpec((None, k, tf), lambda c, f, ce, cs, cn, us: (ce[c], 0, col(c, f, cn))),
            pl.BlockSpec((None, tf, d_out), lambda c, f, ce, cs, cn, us: (ce[c], col(c, f, cn), 0)),
        ],
        out_specs=pl.BlockSpec(memory_space=pl.ANY),
        scratch_shapes=[
            pltpu.VMEM((k, tf), BF16), pltpu.VMEM((k, tf), BF16), pltpu.VMEM((tf, d_out), BF16),
            pltpu.VMEM((chunk_tiles, tm, k), BF16),
            pltpu.VMEM((chunk_tiles, tm, d_out), F32),
            pltpu.VMEM((2, tm, tf), BF16),
            pltpu.VMEM((2, tm, k), x.dtype),
            pltpu.SMEM((1,), jnp.int32),
            pltpu.SemaphoreType.DMA((2,)),
            pltpu.SemaphoreType.DMA(()),
        ],
    )
    return pl.pallas_call(
        functools.partial(_fused_ffn_kernel, tm=tm, n_tiles=m // tm),
        grid_spec=grid_spec,
        out_shape=jax.ShapeDtypeStruct((m, d_out), F32),
        compiler_params=_cparams(("arbitrary", "arbitrary")),
        name=name,
    )(*chunks, x, wg, wu, wd)


def _na_bias_tiles(rpb, rows):
    n_heads, _, n_co = rpb.shape
    c = np.arange(GRID_W)
    c_start = np.clip(c - NA_KW // 2, 0, GRID_W - NA_KW)
    col_in = (c[None, :] >= c_start[:, None]) & (c[None, :] < c_start[:, None] + NA_KW)
    col_off = np.clip(c[None, :] - c[:, None], -(NA_KW - 1), NA_KW - 1) + NA_KW - 1
    pick = (col_off[None] == np.arange(n_co)[:, None, None]).astype(np.float32)
    blocks = jnp.einsum("hrj,jqk->hrqk", rpb.astype(F32) * LOG2E, pick,
                        precision=lax.Precision.HIGHEST)
    blocks = jnp.where(col_in[None, None], blocks, NEG_INF)
    masked = jnp.full((n_heads, GRID_W, GRID_W), NEG_INF, F32)
    tiles = []
    for r0, ks in ((0, 0), (NA_Q_ROWS, NA_Q_ROWS - NA_KH // 2), (rows - NA_Q_ROWS, rows - NA_K_ROWS)):
        q_rows = []
        for qrow in range(r0, r0 + NA_Q_ROWS):
            r_start = min(max(qrow - NA_KH // 2, 0), rows - NA_KH)
            q_rows.append(jnp.concatenate(
                [blocks[:, krow - qrow + NA_KH - 1] if r_start <= krow < r_start + NA_KH else masked
                 for krow in range(ks, ks + NA_K_ROWS)], axis=-1))
        tiles.append(jnp.concatenate(q_rows, axis=1))
    return jnp.stack(tiles, axis=1)


def _na_kernel(q_ref, k_ref, v_ref, b_ref, o_ref, *, rows):
    rb = pl.program_id(2)
    ks = jnp.clip(rb * NA_Q_ROWS - NA_KH // 2, 0, rows - NA_K_ROWS)
    k0 = pl.multiple_of(ks * GRID_W, GRID_W)
    nk = NA_K_ROWS * GRID_W
    kb = k_ref[pl.ds(k0, nk), :]
    vb = v_ref[pl.ds(k0, nk), :]
    s = lax.dot_general(q_ref[...], kb, (((1,), (1,)), ((), ())), preferred_element_type=F32)
    s = s * (HEAD_DIM ** -0.5 * LOG2E) + b_ref[...]
    p = jnp.exp2(s - jnp.max(s, axis=-1, keepdims=True)).astype(BF16)
    v1 = jnp.concatenate([vb, jnp.ones_like(vb)], axis=-1)
    o = jnp.dot(p, v1, preferred_element_type=F32)
    o_ref[...] = (o[:, :HEAD_DIM] / o[:, HEAD_DIM:]).astype(o_ref.dtype)


def neighbourhood_attention(proj, rpb):
    b, s, _ = proj.shape
    rows = s // GRID_W
    n_rb = rows // NA_Q_ROWS
    tq = NA_Q_ROWS * GRID_W
    bias = _na_bias_tiles(rpb, rows)

    def pat(r):
        return jnp.where(r == 0, 0, jnp.where(r == n_rb - 1, 2, 1))

    return pl.pallas_call(
        functools.partial(_na_kernel, rows=rows),
        grid=(b, NA_HEADS, n_rb),
        in_specs=[
            pl.BlockSpec((None, tq, HEAD_DIM), lambda bi, h, r: (bi, r, h)),
            pl.BlockSpec((None, s, HEAD_DIM), lambda bi, h, r: (bi, 0, NA_HEADS + h)),
            pl.BlockSpec((None, s, HEAD_DIM), lambda bi, h, r: (bi, 0, 2 * NA_HEADS + h)),
            pl.BlockSpec((None, None, tq, NA_K_ROWS * GRID_W), lambda bi, h, r: (h, pat(r), 0, 0)),
        ],
        out_specs=pl.BlockSpec((None, tq, HEAD_DIM), lambda bi, h, r: (bi, r, h)),
        out_shape=jax.ShapeDtypeStruct((b, s, NA_HEADS * HEAD_DIM), BF16),
        compiler_params=_cparams(("arbitrary", "arbitrary", "arbitrary")),
        name="na_attention",
    )(proj, proj, proj, bias)


def _rope_tables(seq):
    dq = HEAD_DIM // 2
    inv = 1.0 / (ROPE_THETA ** (jnp.arange(0, dq, 2, dtype=F32) / dq))
    ang = jnp.arange(seq, dtype=F32)[:, None] * inv[None, :]
    cos, sin = jnp.cos(ang), jnp.sin(ang)
    zero = jnp.zeros_like(sin)
    c = jnp.concatenate([cos, cos, cos, cos], axis=-1)
    s_first = jnp.concatenate([-sin, zero, -sin, zero], axis=-1)
    s_second = jnp.concatenate([zero, sin, zero, sin], axis=-1)
    return c, s_first, s_second


def _rope(x, c, s_first, s_second):
    half = HEAD_DIM // 4
    return (x * c + pltpu.roll(x, HEAD_DIM - half, axis=1) * s_first
            + pltpu.roll(x, half, axis=1) * s_second)


def _da_kernel(lam_ref, q_ref, k_ref, v_ref, cq_ref, s1q_ref, s2q_ref, ck_ref, s1k_ref, s2k_ref,
               g_ref, o_ref, kr_ref, v1_ref, *, out_scale):
    s = k_ref.shape[0]

    @pl.when(pl.program_id(2) == 0)
    def _():
        k = k_ref[...].astype(F32)
        kr_ref[...] = _rope(k, ck_ref[...], s1k_ref[...], s2k_ref[...]).astype(BF16)
        v1_ref[:, :HEAD_DIM] = v_ref[...]
        v1_ref[:, HEAD_DIM:] = jnp.ones_like(v_ref)

    dq = HEAD_DIM // 2
    q = _rope(q_ref[...].astype(F32), cq_ref[...], s1q_ref[...], s2q_ref[...]) * (dq ** -0.5 * LOG2E)
    lane = lax.broadcasted_iota(jnp.int32, q.shape, 1)
    dn = (((1,), (1,)), ((), ()))
    kc = min(DA_KEY_CHUNK, s)

    def softmax_av(qm):
        m = jnp.full((qm.shape[0], 1), -jnp.inf, F32)
        acc = jnp.zeros((qm.shape[0], 2 * HEAD_DIM), F32)
        for c0 in range(0, s, kc):
            sc = lax.dot_general(qm, kr_ref[c0:c0 + kc, :], dn, preferred_element_type=F32)
            m_new = jnp.maximum(m, jnp.max(sc, axis=-1, keepdims=True))
            p = jnp.exp2(sc - m_new).astype(BF16)
            acc = jnp.exp2(m - m_new) * acc + jnp.dot(p, v1_ref[c0:c0 + kc, :],
                                                       preferred_element_type=F32)
            m = m_new
        return acc[:, :HEAD_DIM] / acc[:, HEAD_DIM:]

    o = (softmax_av(jnp.where(lane < dq, q, 0.0).astype(BF16))
         - lam_ref[0] * softmax_av(jnp.where(lane >= dq, q, 0.0).astype(BF16)))
    o_ref[...] = (_rms(o, g_ref[...]) * out_scale).astype(o_ref.dtype)


def differential_attention(proj, lam, subln, lambda_init):
    b, s, _ = proj.shape
    tq = min(DA_Q_BLOCK, s)
    base = 3 * NA_HEADS
    c, s_first, s_second = _rope_tables(s)
    qtab = pl.BlockSpec((tq, HEAD_DIM), lambda bi, h, i: (i, 0))
    ktab = pl.BlockSpec((s, HEAD_DIM), lambda bi, h, i: (0, 0))
    return pl.pallas_call(
        functools.partial(_da_kernel, out_scale=1.0 - lambda_init),
        grid=(b, DA_HEADS, s // tq),
        in_specs=[
            pl.BlockSpec(memory_space=pltpu.SMEM),
            pl.BlockSpec((None, tq, HEAD_DIM), lambda bi, h, i: (bi, i, base + h)),
            pl.BlockSpec((None, s, HEAD_DIM), lambda bi, h, i: (bi, 0, base + DA_HEADS + h)),
            pl.BlockSpec((None, s, HEAD_DIM), lambda bi, h, i: (bi, 0, base + 2 * DA_HEADS + h)),
            qtab, qtab, qtab, ktab, ktab, ktab,
            pl.BlockSpec((1, HEAD_DIM), lambda bi, h, i: (0, 0)),
        ],
        out_specs=pl.BlockSpec((None, tq, HEAD_DIM), lambda bi, h, i: (bi, i, h)),
        out_shape=jax.ShapeDtypeStruct((b, s, DA_HEADS * HEAD_DIM), BF16),
        scratch_shapes=[pltpu.VMEM((s, HEAD_DIM), BF16), pltpu.VMEM((s, 2 * HEAD_DIM), BF16)],
        compiler_params=_cparams(("arbitrary", "arbitrary", "arbitrary")),
        name="da_attention",
    )(lam, proj, proj, proj, c, s_first, s_second, c, s_first, s_second, subln.reshape(1, HEAD_DIM))


SCAN_PAD_ROWS = 8


def _shift_rows(x, k):
    return pltpu.roll(x, k % x.shape[0], axis=0)


def _lru_kernel(gate_ref, rec_ref, cw_ref, cb_ref, wa_ref, ba_ref, wx_ref, bx_ref, ap_ref, o_ref,
                u_ref, af_ref, bf_ref, ab_ref, bb_ref):
    s = rec_ref.shape[0]
    seg = s // SUBLANES
    pitch = seg + SCAN_PAD_ROWS

    r = rec_ref[...].astype(F32)
    t_idx = lax.broadcasted_iota(jnp.int32, r.shape, 0)
    cw = cw_ref[...]
    u = r * cw[1:2, :] + cb_ref[...]
    u = u + jnp.where(t_idx >= 1, _shift_rows(r, 1), 0.0) * cw[0:1, :]
    u = u + jnp.where(t_idx < s - 1, _shift_rows(r, -1), 0.0) * cw[2:3, :]
    u = u + jnp.where(t_idx < s - 2, _shift_rows(r, -2), 0.0) * cw[3:4, :]
    u_ref[...] = u

    for d, (a_ref, b_ref) in enumerate(((af_ref, bf_ref), (ab_ref, bb_ref))):
        ap = ap_ref[d]
        softplus = jnp.maximum(-ap, 0.0) + jnp.log1p(jnp.exp(-jnp.abs(ap)))
        wa = wa_ref[d].astype(BF16)
        wx = wx_ref[d].astype(BF16)

        def prep(c, carry, d=d, a_ref=a_ref, b_ref=b_ref, softplus=softplus, wa=wa, wx=wx):
            uc = u_ref[pl.ds(pl.multiple_of(c * seg, seg), seg), :]
            ub = uc.astype(BF16)
            ga = jax.nn.sigmoid(jnp.dot(ub, wa, preferred_element_type=F32) + ba_ref[d])
            gx = jax.nn.sigmoid(jnp.dot(ub, wx, preferred_element_type=F32) + bx_ref[d])
            a = jnp.exp(-LRU_C * ga * softplus)
            r0 = pl.multiple_of(c * pitch, SUBLANES)
            a_ref[pl.ds(r0, seg), :] = a
            b_ref[pl.ds(r0, seg), :] = jnp.sqrt(1.0 - a * a) * (gx * uc)
            return carry

        lax.fori_loop(0, SUBLANES, prep, 0)

    def step(i, carry):
        hf, pf, hb, pb = carry
        rows_f = pl.ds(i, SUBLANES, stride=pitch)
        rows_b = pl.ds(seg - 1 - i, SUBLANES, stride=pitch)
        a_f = af_ref[rows_f, :]
        a_b = ab_ref[rows_b, :]
        hf = a_f * hf + bf_ref[rows_f, :]
        hb = a_b * hb + bb_ref[rows_b, :]
        pf = a_f * pf
        pb = a_b * pb
        bf_ref[rows_f, :] = hf
        af_ref[rows_f, :] = pf
        bb_ref[rows_b, :] = hb
        ab_ref[rows_b, :] = pb
        return hf, pf, hb, pb

    zero = jnp.zeros((SUBLANES, LRU_BLOCK_DIM), F32)
    one = jnp.ones((SUBLANES, LRU_BLOCK_DIM), F32)
    hf, pf, hb, pb = lax.fori_loop(0, seg, step, (zero, one, zero, one), unroll=4)

    cin_f = [jnp.zeros((1, LRU_BLOCK_DIM), F32)]
    for c in range(SUBLANES - 1):
        cin_f.append(pf[c:c + 1] * cin_f[-1] + hf[c:c + 1])
    cin_b = [jnp.zeros((1, LRU_BLOCK_DIM), F32)]
    for c in range(SUBLANES - 1, 0, -1):
        cin_b.insert(0, pb[c:c + 1] * cin_b[0] + hb[c:c + 1])

    for c in range(SUBLANES):
        rows = pl.ds(c * pitch, seg)
        h = (bf_ref[rows, :] + af_ref[rows, :] * cin_f[c]) + (bb_ref[rows, :] + ab_ref[rows, :] * cin_b[c])
        y = jax.nn.gelu(gate_ref[pl.ds(c * seg, seg), :].astype(F32), approximate=True)
        o_ref[pl.ds(c * seg, seg), :] = (y * h).astype(o_ref.dtype)


def recurrent_block(proj, conv_w, conv_b, w_a, b_a, w_x, b_x, a_param):
    b, s, w2 = proj.shape
    w = w2 // 2
    nb = w // LRU_BLOCK_DIM
    c = LRU_BLOCK_DIM
    vec = lambda p: p.reshape(2, 1, w)
    return pl.pallas_call(
        _lru_kernel,
        grid=(b, nb),
        in_specs=[
            pl.BlockSpec((None, s, c), lambda bi, n: (bi, 0, n)),
            pl.BlockSpec((None, s, c), lambda bi, n: (bi, 0, nb + n)),
            pl.BlockSpec((conv_w.shape[0], c), lambda bi, n: (0, n)),
            pl.BlockSpec((1, c), lambda bi, n: (0, n)),
            pl.BlockSpec((2, None, c, c), lambda bi, n: (0, n, 0, 0)),
            pl.BlockSpec((2, 1, c), lambda bi, n: (0, 0, n)),
            pl.BlockSpec((2, None, c, c), lambda bi, n: (0, n, 0, 0)),
            pl.BlockSpec((2, 1, c), lambda bi, n: (0, 0, n)),
            pl.BlockSpec((2, 1, c), lambda bi, n: (0, 0, n)),
        ],
        out_specs=pl.BlockSpec((None, s, c), lambda bi, n: (bi, 0, n)),
        out_shape=jax.ShapeDtypeStruct((b, s, w), BF16),
        scratch_shapes=[pltpu.VMEM((s, c), F32)]
        + [pltpu.VMEM((s + SUBLANES * SCAN_PAD_ROWS, c), F32) for _ in range(4)],
        compiler_params=_cparams(("arbitrary", "arbitrary")),
        name="rg_lru",
    )(proj, proj, conv_w, conv_b.reshape(1, w), w_a, vec(b_a), w_x, vec(b_x), vec(a_param))


R_E0, R_E1, R_G0, R_G1, R_RANK0, R_RANK1 = range(6)


def _route_kernel(x_ref, g_ref, r_ref, meta_ref, cnt_ref, carry_ref):
    i = pl.program_id(0)

    @pl.when(i == 0)
    def _():
        carry_ref[...] = jnp.zeros_like(carry_ref)

    h = _rms(x_ref[...], g_ref[...])
    logits = jnp.dot(h, r_ref[...], preferred_element_type=F32, precision=lax.Precision.HIGHEST)
    tm = logits.shape[0]
    lane = lax.broadcasted_iota(jnp.int32, logits.shape, 1).astype(F32)
    logits = jnp.where(lane < N_EXPERTS, logits, -jnp.inf)
    m0 = jnp.max(logits, axis=-1, keepdims=True)
    e0 = jnp.min(jnp.where(logits == m0, lane, float(LANES)), axis=-1, keepdims=True)
    rest = jnp.where(lane == e0, -jnp.inf, logits)
    m1 = jnp.max(rest, axis=-1, keepdims=True)
    e1 = jnp.min(jnp.where(rest == m1, lane, float(LANES)), axis=-1, keepdims=True)
    z = jnp.exp(m1 - m0)
    g0 = 1.0 / (1.0 + z)
    g1 = z / (1.0 + z)

    hit0 = lane == e0
    hit1 = lane == e1
    both = jnp.where(hit0 | hit1, 1.0, 0.0).astype(BF16)
    ri = lax.broadcasted_iota(jnp.int32, (tm, tm), 0)
    ci = lax.broadcasted_iota(jnp.int32, (tm, tm), 1)
    lower = jnp.where(ci < ri, 1.0, 0.0).astype(BF16)
    before = jnp.dot(lower, both, preferred_element_type=F32) + carry_ref[0:1, :]
    rank0 = jnp.sum(jnp.where(hit0, before, 0.0), axis=-1, keepdims=True)
    rank1 = jnp.sum(jnp.where(hit1, before, 0.0), axis=-1, keepdims=True)
    total = carry_ref[0:1, :] + jnp.sum(both.astype(F32), axis=0, keepdims=True)
    carry_ref[...] = jnp.broadcast_to(total, carry_ref.shape)
    cnt_ref[...] = jnp.broadcast_to(total, cnt_ref.shape)

    rec = jnp.zeros(logits.shape, F32)
    for col, val in ((R_E0, e0), (R_E1, e1), (R_G0, g0), (R_G1, g1),
                     (R_RANK0, rank0), (R_RANK1, rank1)):
        rec = jnp.where(lane == col, val, rec)
    meta_ref[...] = rec


def route(x, g, router):
    t, d = x.shape
    tm = ROUTE_TILE
    r_pad = jnp.zeros((d, LANES), F32).at[:, :N_EXPERTS].set(router)
    return pl.pallas_call(
        _route_kernel,
        grid=(t // tm,),
        in_specs=[pl.BlockSpec((tm, d), lambda i: (i, 0)),
                  pl.BlockSpec((1, d), lambda i: (0, 0)),
                  pl.BlockSpec((d, LANES), lambda i: (0, 0))],
        out_specs=[pl.BlockSpec((tm, LANES), lambda i: (i, 0)),
                   pl.BlockSpec((SUBLANES, LANES), lambda i: (0, 0))],
        out_shape=[jax.ShapeDtypeStruct((t, LANES), F32),
                   jax.ShapeDtypeStruct((SUBLANES, LANES), F32)],
        scratch_shapes=[pltpu.VMEM((SUBLANES, LANES), F32)],
        compiler_params=_cparams(("arbitrary",)),
        name="moe_route",
    )(x, g.reshape(1, d), r_pad)


def _row_copy(src_ref, src_row, dst_ref, dst_row, sem):
    return pltpu.make_async_copy(src_ref.at[pl.ds(src_row, 1), :], dst_ref.at[pl.ds(dst_row, 1), :], sem)


def _tile_wait(src_ref, tile_ref, sem):
    rows = tile_ref.shape[0]
    pltpu.make_async_copy(src_ref.at[pl.ds(0, rows), :], tile_ref, sem).wait()


def _dispatch_kernel(p0_ref, p1_ref, x_ref, g_ref, init_ref, xs_ref, h_ref, sems):
    del init_ref
    tm = x_ref.shape[0]
    base = pl.program_id(0) * tm
    h_ref[...] = _rms(x_ref[...], g_ref[...])

    def start(t, carry):
        _row_copy(h_ref, t, xs_ref, p0_ref[base + t], sems.at[0]).start()
        _row_copy(h_ref, t, xs_ref, p1_ref[base + t], sems.at[1]).start(priority=1)
        return carry

    lax.fori_loop(0, tm, start, 0, unroll=ROW_COPY_UNROLL)
    for k in range(2):
        _tile_wait(xs_ref, h_ref, sems.at[k])


def dispatch(x, g, pos0, pos1, n_rows):
    t, d = x.shape
    tm = ROUTE_TILE
    grid_spec = pltpu.PrefetchScalarGridSpec(
        num_scalar_prefetch=2,
        grid=(t // tm,),
        in_specs=[pl.BlockSpec((tm, d), lambda i, p0, p1: (i, 0)),
                  pl.BlockSpec((1, d), lambda i, p0, p1: (0, 0)),
                  pl.BlockSpec(memory_space=pl.ANY)],
        out_specs=pl.BlockSpec(memory_space=pl.ANY),
        scratch_shapes=[pltpu.VMEM((tm, d), F32), pltpu.SemaphoreType.DMA((2,))],
    )
    return pl.pallas_call(
        _dispatch_kernel,
        grid_spec=grid_spec,
        out_shape=jax.ShapeDtypeStruct((n_rows, d), F32),
        input_output_aliases={4: 0},
        compiler_params=_cparams(("arbitrary",)),
        name="moe_dispatch",
    )(pos0, pos1, x, g.reshape(1, d), jnp.zeros((n_rows, d), F32))


def _combine_kernel(p0_ref, p1_ref, x_ref, meta_ref, g_ref, y_ref, o_ref, y0_ref, y1_ref, sems):
    tm = x_ref.shape[0]
    base = pl.program_id(0) * tm

    def start(t, carry):
        _row_copy(y_ref, p0_ref[base + t], y0_ref, t, sems.at[0]).start()
        _row_copy(y_ref, p1_ref[base + t], y1_ref, t, sems.at[1]).start(priority=1)
        return carry

    lax.fori_loop(0, tm, start, 0, unroll=ROW_COPY_UNROLL)
    _tile_wait(y_ref, y0_ref, sems.at[0])
    _tile_wait(y_ref, y1_ref, sems.at[1])
    meta = meta_ref[...]
    g0 = meta[:, R_G0:R_G0 + 1]
    g1 = meta[:, R_G1:R_G1 + 1]
    out = x_ref[...] + g0 * y0_ref[...] + g1 * y1_ref[...]
    o_ref[...] = _rms(out, g_ref[...])


def combine(x, meta, g, y, pos0, pos1):
    t, d = x.shape
    tm = ROUTE_TILE
    grid_spec = pltpu.PrefetchScalarGridSpec(
        num_scalar_prefetch=2,
        grid=(t // tm,),
        in_specs=[pl.BlockSpec((tm, d), lambda i, p0, p1: (i, 0)),
                  pl.BlockSpec((tm, LANES), lambda i, p0, p1: (i, 0)),
                  pl.BlockSpec((1, d), lambda i, p0, p1: (0, 0)),
                  pl.BlockSpec(memory_space=pl.ANY)],
        out_specs=pl.BlockSpec((tm, d), lambda i, p0, p1: (i, 0)),
        scratch_shapes=[pltpu.VMEM((tm, d), F32), pltpu.VMEM((tm, d), F32),
                        pltpu.SemaphoreType.DMA((2,))],
    )
    return pl.pallas_call(
        _combine_kernel,
        grid_spec=grid_spec,
        out_shape=jax.ShapeDtypeStruct((t, d), F32),
        compiler_params=_cparams(("arbitrary",)),
        name="moe_combine",
    )(pos0, pos1, x, meta, g.reshape(1, d), y)


def _moe_plan(meta, counts, tm):
    counts = counts[0, :N_EXPERTS].astype(jnp.int32)
    group_tiles = (counts + tm - 1) // tm
    group_start = jnp.cumsum(group_tiles) - group_tiles
    e0 = meta[:, R_E0].astype(jnp.int32)
    e1 = meta[:, R_E1].astype(jnp.int32)
    pos0 = group_start[e0] * tm + meta[:, R_RANK0].astype(jnp.int32)
    pos1 = group_start[e1] * tm + meta[:, R_RANK1].astype(jnp.int32)
    return pos0, pos1, group_start, group_tiles


def _moe_chunks(group_start, group_tiles, chunk_tiles, n_chunks):
    per = (group_tiles + chunk_tiles - 1) // chunk_tiles
    c_end = jnp.cumsum(per)
    c_start = c_end - per
    total = c_end[-1]
    c = jnp.arange(n_chunks, dtype=jnp.int32)
    c_eff = jnp.minimum(c, jnp.maximum(total - 1, 0))
    expert = jnp.sum((c_eff[:, None] >= c_end[None, :]).astype(jnp.int32), axis=1)
    expert = jnp.minimum(expert, N_EXPERTS - 1)
    k = c_eff - c_start[expert]
    first_tile = group_start[expert] + k * chunk_tiles
    tiles = jnp.clip(group_tiles[expert] - k * chunk_tiles, 0, chunk_tiles)
    tiles = jnp.where(c < total, tiles, 0)
    used = (group_start[-1] + group_tiles[-1]).reshape(1)
    return expert, first_tile, tiles, used


def kernel(x, ev_mix_norm, ev_w_in, ev_na_rpb, ev_da_lambda_q1, ev_da_lambda_k1, ev_da_lambda_q2, ev_da_lambda_k2, ev_da_subln, ev_w_out, ev_ffn_norm, ev_ffn_w_gate, ev_ffn_w_up, ev_ffn_w_down, od_mix_norm, od_w_in, od_conv_w, od_conv_b, od_lru_w_a, od_lru_b_a, od_lru_w_x, od_lru_b_x, od_lru_a_param, od_w_out, od_ffn_norm, od_router, od_moe_w_gate, od_moe_w_up, od_moe_w_down, final_norm):
    b, s, d = x.shape
    t = b * s
    xt = x.reshape(t, d)

    lambda_init = 0.8 - 0.6 * math.exp(-0.3 * 0)
    h = rmsnorm(xt, ev_mix_norm[0], BF16)
    proj = ws_matmul(h, [ev_w_in], tm=512, tn=1024, epilogue="cast", out_dtype=BF16, name="l0_in_proj")
    proj = proj.reshape(b, s, -1)
    na_out = neighbourhood_attention(proj, ev_na_rpb[0])
    lam = (jnp.exp(jnp.sum(ev_da_lambda_q1[0] * ev_da_lambda_k1[0]))
           - jnp.exp(jnp.sum(ev_da_lambda_q2[0] * ev_da_lambda_k2[0])) + lambda_init).reshape(1)
    da_out = differential_attention(proj, lam, ev_da_subln[0], lambda_init)
    mix = jnp.concatenate([na_out, da_out], axis=-1).reshape(t, -1)
    xt = ws_matmul(mix, [ev_w_out], tm=512, tn=1024, epilogue="residual", out_dtype=F32, res=xt,
                   name="l0_out_proj")

    h = rmsnorm(xt, ev_ffn_norm[0], BF16)
    act = ws_matmul(h, [ev_ffn_w_gate, ev_ffn_w_up], tm=512, tn=1024, epilogue="swiglu",
                    out_dtype=BF16, name="l0_ffn_up")
    xt = ws_matmul(act, [ev_ffn_w_down], tm=256, tn=512, epilogue="residual", out_dtype=F32, res=xt,
                   name="l0_ffn_down")

    h = rmsnorm(xt, od_mix_norm[0], BF16)
    proj = ws_matmul(h, [od_w_in], tm=512, tn=1024, epilogue="cast", out_dtype=BF16, name="l1_in_proj")
    rec = recurrent_block(proj.reshape(b, s, -1), od_conv_w[0], od_conv_b[0], od_lru_w_a[0],
                          od_lru_b_a[0], od_lru_w_x[0], od_lru_b_x[0], od_lru_a_param[0])
    xt = ws_matmul(rec.reshape(t, -1), [od_w_out], tm=512, tn=1024, epilogue="residual",
                   out_dtype=F32, res=xt, name="l1_out_proj")

    tm = MOE_ROW_TILE
    n_tiles = (t * 2) // tm + N_EXPERTS
    meta, counts = route(xt, od_ffn_norm[0], od_router[0])
    pos0, pos1, group_start, group_tiles = _moe_plan(meta, counts, tm)
    xs = dispatch(xt, od_ffn_norm[0], pos0, pos1, n_tiles * tm)
    n_chunks = (n_tiles - N_EXPERTS) // MOE_CHUNK_TILES + N_EXPERTS
    chunks = _moe_chunks(group_start, group_tiles, MOE_CHUNK_TILES, n_chunks)
    y = fused_expert_swiglu(xs, od_moe_w_gate[0], od_moe_w_up[0], od_moe_w_down[0], chunks, tm=tm,
                            chunk_tiles=MOE_CHUNK_TILES, tf=FFN_COL_TILE, name="moe_ffn")
    out = combine(xt, meta, final_norm, y, pos0, pos1)
    return out.reshape(b, s, d)
```

```python
import functools
import math

import numpy as np
import jax
import jax.numpy as jnp
from jax import lax
from jax.experimental import pallas as pl
from jax.experimental.pallas import tpu as pltpu

F32 = jnp.float32
BF16 = jnp.bfloat16

HEAD_DIM = 128
NA_HEADS = 8
DA_HEADS = 8
GRID_W = 64
NA_KH = 8
NA_KW = 16
ROPE_THETA = 10000.0
LRU_BLOCK_DIM = 128
LRU_C = 8.0
N_EXPERTS = 8
EPS = 1e-6
NEG_INF = -1e30
LOG2E = math.log2(math.e)

NA_Q_ROWS = 8
NA_K_ROWS = NA_Q_ROWS + NA_KH
DA_Q_BLOCK = 512
DA_KEY_CHUNK = 256
ROUTE_TILE = 256
MOE_ROW_TILE = ROUTE_TILE
MOE_CHUNK_TILES = 9
FFN_COL_TILE = 256
ROW_COPY_UNROLL = 8
LANES = 128
SUBLANES = 8
VMEM_LIMIT = 56 * 1024 * 1024


def _cparams(semantics):
    return pltpu.CompilerParams(dimension_semantics=semantics, vmem_limit_bytes=VMEM_LIMIT)


def _rms(x, g):
    return x * lax.rsqrt(jnp.mean(x * x, axis=-1, keepdims=True) + EPS) * g


def _rmsnorm_kernel(x_ref, g_ref, o_ref):
    o_ref[...] = _rms(x_ref[...], g_ref[...]).astype(o_ref.dtype)


def rmsnorm(x, g, out_dtype, tm=512):
    t, d = x.shape
    return pl.pallas_call(
        _rmsnorm_kernel,
        grid=(t // tm,),
        in_specs=[pl.BlockSpec((tm, d), lambda i: (i, 0)),
                  pl.BlockSpec((1, d), lambda i: (0, 0))],
        out_specs=pl.BlockSpec((tm, d), lambda i: (i, 0)),
        out_shape=jax.ShapeDtypeStruct((t, d), out_dtype),
        compiler_params=_cparams(("arbitrary",)),
        name="rmsnorm",
    )(x, g.reshape(1, d))


CAST_ROWS = 256


def _cast_weight(w_ref, wbf_ref):
    k = w_ref.shape[0]
    rows = min(CAST_ROWS, k)
    assert k % rows == 0

    def body(c, carry):
        r = pl.multiple_of(c * rows, rows)
        wbf_ref[pl.ds(r, rows), :] = w_ref[pl.ds(r, rows), :].astype(BF16)
        return carry

    lax.fori_loop(0, k // rows, body, 0)


def _mm_kernel(a_ref, *refs, n_w, epilogue):
    w_refs = refs[:n_w]
    pos = n_w
    res_ref = None
    if epilogue == "residual":
        res_ref = refs[pos]
        pos += 1
    o_ref = refs[pos]
    wbf_refs = refs[pos + 1:pos + 1 + n_w]

    @pl.when(pl.program_id(1) == 0)
    def _():
        for w_ref, wbf_ref in zip(w_refs, wbf_refs):
            _cast_weight(w_ref, wbf_ref)

    a = a_ref[...].astype(BF16)
    acc = jnp.dot(a, wbf_refs[0][...], preferred_element_type=F32)
    if epilogue == "swiglu":
        up = jnp.dot(a, wbf_refs[1][...], preferred_element_type=F32)
        acc = acc * jax.nn.sigmoid(acc) * up
    elif epilogue == "residual":
        acc = acc + res_ref[...]
    o_ref[...] = acc.astype(o_ref.dtype)


def ws_matmul(a, ws, *, tm, tn, epilogue, out_dtype, res=None, name):
    m, k = a.shape
    n = ws[0].shape[-1]
    n_w = len(ws)
    in_specs = [pl.BlockSpec((tm, k), lambda j, i: (i, 0))]
    in_specs += [pl.BlockSpec((None, k, tn), lambda j, i: (0, 0, j)) for _ in ws]
    operands = [a, *ws]
    if epilogue == "residual":
        in_specs.append(pl.BlockSpec((tm, tn), lambda j, i: (i, j)))
        operands.append(res)
    return pl.pallas_call(
        functools.partial(_mm_kernel, n_w=n_w, epilogue=epilogue),
        grid=(n // tn, m // tm),
        in_specs=in_specs,
        out_specs=pl.BlockSpec((tm, tn), lambda j, i: (i, j)),
        out_shape=jax.ShapeDtypeStruct((m, n), out_dtype),
        scratch_shapes=[pltpu.VMEM((k, tn), BF16) for _ in ws],
        compiler_params=_cparams(("arbitrary", "arbitrary")),
        name=name,
    )(*operands)


def _silu_mul(g, u):
    return g * jax.nn.sigmoid(g) * u


def _fused_ffn_kernel(ce_ref, cs_ref, cn_ref, used_ref, x_hbm, wg_hbm, wu_hbm, wd_hbm, y_hbm,
                      wg_st, wu_st, wd_st, wg_bf, wu_bf, wd_bf, xbuf, acc, hbuf, stage, pend,
                      w_sem, x_sem, y_sem, *, tm, n_tiles):
    c = pl.program_id(0)
    f = pl.program_id(1)
    last_c = pl.num_programs(0) - 1
    last_f = pl.num_programs(1) - 1
    n = cn_ref[c]
    t0 = cs_ref[c]
    tf = wd_st.shape[1]

    def w_copies(chunk, fcol, slot):
        e = ce_ref[chunk]
        col = pl.multiple_of(fcol * tf, tf)
        return (
            (pltpu.make_async_copy(wg_hbm.at[e, :, pl.ds(col, tf)], wg_st.at[slot], w_sem.at[0, slot]), 0),
            (pltpu.make_async_copy(wu_hbm.at[e, :, pl.ds(col, tf)], wu_st.at[slot], w_sem.at[1, slot]), 1),
            (pltpu.make_async_copy(wd_hbm.at[e, pl.ds(col, tf), :], wd_st.at[slot], w_sem.at[2, slot]), 0),
        )

    def w_start(chunk, fcol, slot):
        for copy, priority in w_copies(chunk, fcol, slot):
            copy.start(priority=priority)

    def x_copy(t, slot):
        row = pl.multiple_of((t0 + t) * tm, tm)
        return pltpu.make_async_copy(x_hbm.at[pl.ds(row, tm), :], stage.at[slot], x_sem.at[slot])

    def y_copy(tile, t):
        row = pl.multiple_of(tile * tm, tm)
        return pltpu.make_async_copy(acc.at[t], y_hbm.at[pl.ds(row, tm), :], y_sem)

    def drain_y():
        def wait(t, carry):
            y_copy(0, 0).wait()
            return carry

        lax.fori_loop(0, pend[0], wait, 0)
        pend[0] = 0

    @pl.when((c == 0) & (f == 0))
    def _():
        pend[0] = 0

    @pl.when((n > 0) & (f == 0))
    def _():
        x_copy(0, 0).start()
        drain_y()

        def load(t, carry):
            slot = lax.rem(t, 2)

            @pl.when(t + 1 < n)
            def _():
                x_copy(t + 1, 1 - slot).start()

            x_copy(t, slot).wait()
            xbuf[t] = stage[slot].astype(BF16)
            acc[t] = jnp.zeros(acc.shape[1:], F32)
            return carry

        lax.fori_loop(0, n, load, 0)

    @pl.when(n > 0)
    def _():
        slot = lax.rem(f, 2)
        nxt = jnp.minimum(c + 1, last_c)

        @pl.when((c == 0) & (f == 0))
        def _():
            w_start(c, f, 0)

        @pl.when(f < last_f)
        def _():
            w_start(c, f + 1, 1 - slot)

        @pl.when((f == last_f) & (c < last_c) & (cn_ref[nxt] > 0))
        def _():
            w_start(nxt, 0, 0)

        for copy, _ in w_copies(c, f, slot):
            copy.wait()
        for w_st, wbf_ref in ((wg_st, wg_bf), (wu_st, wu_bf), (wd_st, wd_bf)):
            _cast_weight(w_st.at[slot], wbf_ref)

        def up(t):
            a = xbuf[t]
            g = jnp.dot(a, wg_bf[...], preferred_element_type=F32)
            u = jnp.dot(a, wu_bf[...], preferred_element_type=F32)
            return _silu_mul(g, u).astype(BF16)

        def down(t, h):
            acc[t] = acc[t] + jnp.dot(h, wd_bf[...], preferred_element_type=F32)

        hbuf[0] = up(0)

        def body(t, carry):
            h_prev = hbuf[lax.rem(t - 1, 2)]
            h = up(t)
            down(t - 1, h_prev)
            hbuf[lax.rem(t, 2)] = h
            return carry

        lax.fori_loop(1, n, body, 0)
        down(n - 1, hbuf[lax.rem(n - 1, 2)])

    @pl.when((n > 0) & (f == last_f))
    def _():
        def store(t, carry):
            y_copy(t0 + t, t).start()
            return carry

        lax.fori_loop(0, n, store, 0)
        pend[0] = n

    @pl.when((c == last_c) & (f == last_f))
    def _():
        drain_y()
        acc[0] = jnp.zeros(acc.shape[1:], F32)

        def fill(tile, carry):
            y_copy(tile, 0).start()
            y_copy(tile, 0).wait()
            return carry

        lax.fori_loop(used_ref[0], n_tiles, fill, 0)


def fused_expert_swiglu(x, wg, wu, wd, chunks, *, tm, chunk_tiles, tf, name):
    m, k = x.shape
    _, _, d_ff = wg.shape
    d_out = wd.shape[-1]
    n_chunks = chunks[0].shape[0]
    n_f = d_ff // tf

    assert n_f % 2 == 0
    grid_spec = pltpu.PrefetchScalarGridSpec(
        num_scalar_prefetch=4,
        grid=(n_chunks, n_f),
        in_specs=[pl.BlockSpec(memory_space=pl.ANY) for _ in range(4)],
        out_specs=pl.BlockSpec(memory_space=pl.ANY),
        scratch_shapes=[
            pltpu.VMEM((2, k, tf), F32), pltpu.VMEM((2, k, tf), F32), pltpu.VMEM((2, tf, d_out), F32),
            pltpu.VMEM((k, tf), BF16), pltpu.VMEM((k, tf), BF16), pltpu.VMEM((tf, d_out), BF16),
            pltpu.VMEM((chunk_tiles, tm, k), BF16),
            pltpu.VMEM((chunk_tiles, tm, d_out), F32),
            pltpu.VMEM((2, tm, tf), BF16),
            pltpu.VMEM((2, tm, k), x.dtype),
            pltpu.SMEM((1,), jnp.int32),
            pltpu.SemaphoreType.DMA((3, 2)),
            pltpu.SemaphoreType.DMA((2,)),
            pltpu.SemaphoreType.DMA(()),
        ],
    )
    return pl.pallas_call(
        functools.partial(_fused_ffn_kernel, tm=tm, n_tiles=m // tm),
        grid_spec=grid_spec,
        out_shape=jax.ShapeDtypeStruct((m, d_out), F32),
        compiler_params=_cparams(("arbitrary", "arbitrary")),
        name=name,
    )(*chunks, x, wg, wu, wd)


def _na_bias_tiles(rpb, rows):
    n_heads, _, n_co = rpb.shape
    c = np.arange(GRID_W)
    c_start = np.clip(c - NA_KW // 2, 0, GRID_W - NA_KW)
    col_in = (c[None, :] >= c_start[:, None]) & (c[None, :] < c_start[:, None] + NA_KW)
    col_off = np.clip(c[None, :] - c[:, None], -(NA_KW - 1), NA_KW - 1) + NA_KW - 1
    pick = (col_off[None] == np.arange(n_co)[:, None, None]).astype(np.float32)
    blocks = jnp.einsum("hrj,jqk->hrqk", rpb.astype(F32) * LOG2E, pick,
                        precision=lax.Precision.HIGHEST)
    blocks = jnp.where(col_in[None, None], blocks, NEG_INF)
    masked = jnp.full((n_heads, GRID_W, GRID_W), NEG_INF, F32)
    tiles = []
    for r0, ks in ((0, 0), (NA_Q_ROWS, NA_Q_ROWS - NA_KH // 2), (rows - NA_Q_ROWS, rows - NA_K_ROWS)):
        q_rows = []
        for qrow in range(r0, r0 + NA_Q_ROWS):
            r_start = min(max(qrow - NA_KH // 2, 0), rows - NA_KH)
            q_rows.append(jnp.concatenate(
                [blocks[:, krow - qrow + NA_KH - 1] if r_start <= krow < r_start + NA_KH else masked
                 for krow in range(ks, ks + NA_K_ROWS)], axis=-1))
        tiles.append(jnp.concatenate(q_rows, axis=1))
    return jnp.stack(tiles, axis=1)


def _na_kernel(q_ref, k_ref, v_ref, b_ref, o_ref, *, rows):
    rb = pl.program_id(2)
    ks = jnp.clip(rb * NA_Q_ROWS - NA_KH // 2, 0, rows - NA_K_ROWS)
    k0 = pl.multiple_of(ks * GRID_W, GRID_W)
    nk = NA_K_ROWS * GRID_W
    kb = k_ref[pl.ds(k0, nk), :]
    vb = v_ref[pl.ds(k0, nk), :]
    s = lax.dot_general(q_ref[...], kb, (((1,), (1,)), ((), ())), preferred_element_type=F32)
    s = s * (HEAD_DIM ** -0.5 * LOG2E) + b_ref[...]
    p = jnp.exp2(s - jnp.max(s, axis=-1, keepdims=True)).astype(BF16)
    v1 = jnp.concatenate([vb, jnp.ones_like(vb)], axis=-1)
    o = jnp.dot(p, v1, preferred_element_type=F32)
    o_ref[...] = (o[:, :HEAD_DIM] / o[:, HEAD_DIM:]).astype(o_ref.dtype)


def neighbourhood_attention(proj, rpb):
    b, s, _ = proj.shape
    rows = s // GRID_W
    n_rb = rows // NA_Q_ROWS
    tq = NA_Q_ROWS * GRID_W
    bias = _na_bias_tiles(rpb, rows)

    def pat(r):
        return jnp.where(r == 0, 0, jnp.where(r == n_rb - 1, 2, 1))

    return pl.pallas_call(
        functools.partial(_na_kernel, rows=rows),
        grid=(b, NA_HEADS, n_rb),
        in_specs=[
            pl.BlockSpec((None, tq, HEAD_DIM), lambda bi, h, r: (bi, r, h)),
            pl.BlockSpec((None, s, HEAD_DIM), lambda bi, h, r: (bi, 0, NA_HEADS + h)),
            pl.BlockSpec((None, s, HEAD_DIM), lambda bi, h, r: (bi, 0, 2 * NA_HEADS + h)),
            pl.BlockSpec((None, None, tq, NA_K_ROWS * GRID_W), lambda bi, h, r: (h, pat(r), 0, 0)),
        ],
        out_specs=pl.BlockSpec((None, tq, HEAD_DIM), lambda bi, h, r: (bi, r, h)),
        out_shape=jax.ShapeDtypeStruct((b, s, NA_HEADS * HEAD_DIM), BF16),
        compiler_params=_cparams(("arbitrary", "arbitrary", "arbitrary")),
        name="na_attention",
    )(proj, proj, proj, bias)


def _rope_tables(seq):
    dq = HEAD_DIM // 2
    inv = 1.0 / (ROPE_THETA ** (jnp.arange(0, dq, 2, dtype=F32) / dq))
    ang = jnp.arange(seq, dtype=F32)[:, None] * inv[None, :]
    cos, sin = jnp.cos(ang), jnp.sin(ang)
    zero = jnp.zeros_like(sin)
    c = jnp.concatenate([cos, cos, cos, cos], axis=-1)
    s_first = jnp.concatenate([-sin, zero, -sin, zero], axis=-1)
    s_second = jnp.concatenate([zero, sin, zero, sin], axis=-1)
    return c, s_first, s_second


def _rope(x, c, s_first, s_second):
    half = HEAD_DIM // 4
    return (x * c + pltpu.roll(x, HEAD_DIM - half, axis=1) * s_first
            + pltpu.roll(x, half, axis=1) * s_second)


def _da_kernel(lam_ref, q_ref, k_ref, v_ref, cq_ref, s1q_ref, s2q_ref, ck_ref, s1k_ref, s2k_ref,
               g_ref, o_ref, kr_ref, v1_ref, *, out_scale):
    s = k_ref.shape[0]

    @pl.when(pl.program_id(2) == 0)
    def _():
        k = k_ref[...].astype(F32)
        kr_ref[...] = _rope(k, ck_ref[...], s1k_ref[...], s2k_ref[...]).astype(BF16)
        v1_ref[:, :HEAD_DIM] = v_ref[...]
        v1_ref[:, HEAD_DIM:] = jnp.ones_like(v_ref)

    dq = HEAD_DIM // 2
    q = _rope(q_ref[...].astype(F32), cq_ref[...], s1q_ref[...], s2q_ref[...]) * (dq ** -0.5 * LOG2E)
    lane = lax.broadcasted_iota(jnp.int32, q.shape, 1)
    dn = (((1,), (1,)), ((), ()))
    kc = min(DA_KEY_CHUNK, s)

    def softmax_av(qm):
        m = jnp.full((qm.shape[0], 1), -jnp.inf, F32)
        acc = jnp.zeros((qm.shape[0], 2 * HEAD_DIM), F32)
        for c0 in range(0, s, kc):
            sc = lax.dot_general(qm, kr_ref[c0:c0 + kc, :], dn, preferred_element_type=F32)
            m_new = jnp.maximum(m, jnp.max(sc, axis=-1, keepdims=True))
            p = jnp.exp2(sc - m_new).astype(BF16)
            acc = jnp.exp2(m - m_new) * acc + jnp.dot(p, v1_ref[c0:c0 + kc, :],
                                                       preferred_element_type=F32)
            m = m_new
        return acc[:, :HEAD_DIM] / acc[:, HEAD_DIM:]

    o = (softmax_av(jnp.where(lane < dq, q, 0.0).astype(BF16))
         - lam_ref[0] * softmax_av(jnp.where(lane >= dq, q, 0.0).astype(BF16)))
    o_ref[...] = (_rms(o, g_ref[...]) * out_scale).astype(o_ref.dtype)


def differential_attention(proj, lam, subln, lambda_init):
    b, s, _ = proj.shape
    tq = min(DA_Q_BLOCK, s)
    base = 3 * NA_HEADS
    c, s_first, s_second = _rope_tables(s)
    qtab = pl.BlockSpec((tq, HEAD_DIM), lambda bi, h, i: (i, 0))
    ktab = pl.BlockSpec((s, HEAD_DIM), lambda bi, h, i: (0, 0))
    return pl.pallas_call(
        functools.partial(_da_kernel, out_scale=1.0 - lambda_init),
        grid=(b, DA_HEADS, s // tq),
        in_specs=[
            pl.BlockSpec(memory_space=pltpu.SMEM),
            pl.BlockSpec((None, tq, HEAD_DIM), lambda bi, h, i: (bi, i, base + h)),
            pl.BlockSpec((None, s, HEAD_DIM), lambda bi, h, i: (bi, 0, base + DA_HEADS + h)),
            pl.BlockSpec((None, s, HEAD_DIM), lambda bi, h, i: (bi, 0, base + 2 * DA_HEADS + h)),
            qtab, qtab, qtab, ktab, ktab, ktab,
            pl.BlockSpec((1, HEAD_DIM), lambda bi, h, i: (0, 0)),
        ],
        out_specs=pl.BlockSpec((None, tq, HEAD_DIM), lambda bi, h, i: (bi, i, h)),
        out_shape=jax.ShapeDtypeStruct((b, s, DA_HEADS * HEAD_DIM), BF16),
        scratch_shapes=[pltpu.VMEM((s, HEAD_DIM), BF16), pltpu.VMEM((s, 2 * HEAD_DIM), BF16)],
        compiler_params=_cparams(("arbitrary", "arbitrary", "arbitrary")),
        name="da_attention",
    )(lam, proj, proj, proj, c, s_first, s_second, c, s_first, s_second, subln.reshape(1, HEAD_DIM))


SCAN_PAD_ROWS = 8


def _shift_rows(x, k):
    return pltpu.roll(x, k % x.shape[0], axis=0)


def _lru_kernel(gate_ref, rec_ref, cw_ref, cb_ref, wa_ref, ba_ref, wx_ref, bx_ref, ap_ref, o_ref,
                u_ref, af_ref, bf_ref, ab_ref, bb_ref):
    s = rec_ref.shape[0]
    seg = s // SUBLANES
    pitch = seg + SCAN_PAD_ROWS

    r = rec_ref[...].astype(F32)
    t_idx = lax.broadcasted_iota(jnp.int32, r.shape, 0)
    cw = cw_ref[...]
    u = r * cw[1:2, :] + cb_ref[...]
    u = u + jnp.where(t_idx >= 1, _shift_rows(r, 1), 0.0) * cw[0:1, :]
    u = u + jnp.where(t_idx < s - 1, _shift_rows(r, -1), 0.0) * cw[2:3, :]
    u = u + jnp.where(t_idx < s - 2, _shift_rows(r, -2), 0.0) * cw[3:4, :]
    u_ref[...] = u

    for d, (a_ref, b_ref) in enumerate(((af_ref, bf_ref), (ab_ref, bb_ref))):
        ap = ap_ref[d]
        softplus = jnp.maximum(-ap, 0.0) + jnp.log1p(jnp.exp(-jnp.abs(ap)))
        wa = wa_ref[d].astype(BF16)
        wx = wx_ref[d].astype(BF16)

        def prep(c, carry, d=d, a_ref=a_ref, b_ref=b_ref, softplus=softplus, wa=wa, wx=wx):
            uc = u_ref[pl.ds(pl.multiple_of(c * seg, seg), seg), :]
            ub = uc.astype(BF16)
            ga = jax.nn.sigmoid(jnp.dot(ub, wa, preferred_element_type=F32) + ba_ref[d])
            gx = jax.nn.sigmoid(jnp.dot(ub, wx, preferred_element_type=F32) + bx_ref[d])
            a = jnp.exp(-LRU_C * ga * softplus)
            r0 = pl.multiple_of(c * pitch, SUBLANES)
            a_ref[pl.ds(r0, seg), :] = a
            b_ref[pl.ds(r0, seg), :] = jnp.sqrt(1.0 - a * a) * (gx * uc)
            return carry

        lax.fori_loop(0, SUBLANES, prep, 0)

    def step(i, carry):
        hf, pf, hb, pb = carry
        rows_f = pl.ds(i, SUBLANES, stride=pitch)
        rows_b = pl.ds(seg - 1 - i, SUBLANES, stride=pitch)
        a_f = af_ref[rows_f, :]
        a_b = ab_ref[rows_b, :]
        hf = a_f * hf + bf_ref[rows_f, :]
        hb = a_b * hb + bb_ref[rows_b, :]
        pf = a_f * pf
        pb = a_b * pb
        bf_ref[rows_f, :] = hf
        af_ref[rows_f, :] = pf
        bb_ref[rows_b, :] = hb
        ab_ref[rows_b, :] = pb
        return hf, pf, hb, pb

    zero = jnp.zeros((SUBLANES, LRU_BLOCK_DIM), F32)
    one = jnp.ones((SUBLANES, LRU_BLOCK_DIM), F32)
    hf, pf, hb, pb = lax.fori_loop(0, seg, step, (zero, one, zero, one), unroll=4)

    cin_f = [jnp.zeros((1, LRU_BLOCK_DIM), F32)]
    for c in range(SUBLANES - 1):
        cin_f.append(pf[c:c + 1] * cin_f[-1] + hf[c:c + 1])
    cin_b = [jnp.zeros((1, LRU_BLOCK_DIM), F32)]
    for c in range(SUBLANES - 1, 0, -1):
        cin_b.insert(0, pb[c:c + 1] * cin_b[0] + hb[c:c + 1])

    for c in range(SUBLANES):
        rows = pl.ds(c * pitch, seg)
        h = (bf_ref[rows, :] + af_ref[rows, :] * cin_f[c]) + (bb_ref[rows, :] + ab_ref[rows, :] * cin_b[c])
        y = jax.nn.gelu(gate_ref[pl.ds(c * seg, seg), :].astype(F32), approximate=True)
        o_ref[pl.ds(c * seg, seg), :] = (y * h).astype(o_ref.dtype)


def recurrent_block(proj, conv_w, conv_b, w_a, b_a, w_x, b_x, a_param):
    b, s, w2 = proj.shape
    w = w2 // 2
    nb = w // LRU_BLOCK_DIM
    c = LRU_BLOCK_DIM
    vec = lambda p: p.reshape(2, 1, w)
    return pl.pallas_call(
        _lru_kernel,
        grid=(b, nb),
        in_specs=[
            pl.BlockSpec((None, s, c), lambda bi, n: (bi, 0, n)),
            pl.BlockSpec((None, s, c), lambda bi, n: (bi, 0, nb + n)),
            pl.BlockSpec((conv_w.shape[0], c), lambda bi, n: (0, n)),
            pl.BlockSpec((1, c), lambda bi, n: (0, n)),
            pl.BlockSpec((2, None, c, c), lambda bi, n: (0, n, 0, 0)),
            pl.BlockSpec((2, 1, c), lambda bi, n: (0, 0, n)),
            pl.BlockSpec((2, None, c, c), lambda bi, n: (0, n, 0, 0)),
            pl.BlockSpec((2, 1, c), lambda bi, n: (0, 0, n)),
            pl.BlockSpec((2, 1, c), lambda bi, n: (0, 0, n)),
        ],
        out_specs=pl.BlockSpec((None, s, c), lambda bi, n: (bi, 0, n)),
        out_shape=jax.ShapeDtypeStruct((b, s, w), BF16),
        scratch_shapes=[pltpu.VMEM((s, c), F32)]
        + [pltpu.VMEM((s + SUBLANES * SCAN_PAD_ROWS, c), F32) for _ in range(4)],
        compiler_params=_cparams(("arbitrary", "arbitrary")),
        name="rg_lru",
    )(proj, proj, conv_w, conv_b.reshape(1, w), w_a, vec(b_a), w_x, vec(b_x), vec(a_param))


R_E0, R_E1, R_G0, R_G1, R_RANK0, R_RANK1 = range(6)


def _route_kernel(x_ref, g_ref, r_ref, meta_ref, cnt_ref, carry_ref):
    i = pl.program_id(0)

    @pl.when(i == 0)
    def _():
        carry_ref[...] = jnp.zeros_like(carry_ref)

    h = _rms(x_ref[...], g_ref[...])
    logits = jnp.dot(h, r_ref[...], preferred_element_type=F32, precision=lax.Precision.HIGHEST)
    tm = logits.shape[0]
    lane = lax.broadcasted_iota(jnp.int32, logits.shape, 1).astype(F32)
    logits = jnp.where(lane < N_EXPERTS, logits, -jnp.inf)
    m0 = jnp.max(logits, axis=-1, keepdims=True)
    e0 = jnp.min(jnp.where(logits == m0, lane, float(LANES)), axis=-1, keepdims=True)
    rest = jnp.where(lane == e0, -jnp.inf, logits)
    m1 = jnp.max(rest, axis=-1, keepdims=True)
    e1 = jnp.min(jnp.where(rest == m1, lane, float(LANES)), axis=-1, keepdims=True)
    z = jnp.exp(m1 - m0)
    g0 = 1.0 / (1.0 + z)
    g1 = z / (1.0 + z)

    hit0 = lane == e0
    hit1 = lane == e1
    both = jnp.where(hit0 | hit1, 1.0, 0.0).astype(BF16)
    ri = lax.broadcasted_iota(jnp.int32, (tm, tm), 0)
    ci = lax.broadcasted_iota(jnp.int32, (tm, tm), 1)
    lower = jnp.where(ci < ri, 1.0, 0.0).astype(BF16)
    before = jnp.dot(lower, both, preferred_element_type=F32) + carry_ref[0:1, :]
    rank0 = jnp.sum(jnp.where(hit0, before, 0.0), axis=-1, keepdims=True)
    rank1 = jnp.sum(jnp.where(hit1, before, 0.0), axis=-1, keepdims=True)
    total = carry_ref[0:1, :] + jnp.sum(both.astype(F32), axis=0, keepdims=True)
    carry_ref[...] = jnp.broadcast_to(total, carry_ref.shape)
    cnt_ref[...] = jnp.broadcast_to(total, cnt_ref.shape)

    rec = jnp.zeros(logits.shape, F32)
    for col, val in ((R_E0, e0), (R_E1, e1), (R_G0, g0), (R_G1, g1),
                     (R_RANK0, rank0), (R_RANK1, rank1)):
        rec = jnp.where(lane == col, val, rec)
    meta_ref[...] = rec


def route(x, g, router):
    t, d = x.shape
    tm = ROUTE_TILE
    r_pad = jnp.zeros((d, LANES), F32).at[:, :N_EXPERTS].set(router)
    return pl.pallas_call(
        _route_kernel,
        grid=(t // tm,),
        in_specs=[pl.BlockSpec((tm, d), lambda i: (i, 0)),
                  pl.BlockSpec((1, d), lambda i: (0, 0)),
                  pl.BlockSpec((d, LANES), lambda i: (0, 0))],
        out_specs=[pl.BlockSpec((tm, LANES), lambda i: (i, 0)),
                   pl.BlockSpec((SUBLANES, LANES), lambda i: (0, 0))],
        out_shape=[jax.ShapeDtypeStruct((t, LANES), F32),
                   jax.ShapeDtypeStruct((SUBLANES, LANES), F32)],
        scratch_shapes=[pltpu.VMEM((SUBLANES, LANES), F32)],
        compiler_params=_cparams(("arbitrary",)),
        name="moe_route",
    )(x, g.reshape(1, d), r_pad)


def _row_copy(src_ref, src_row, dst_ref, dst_row, sem):
    return pltpu.make_async_copy(src_ref.at[pl.ds(src_row, 1), :], dst_ref.at[pl.ds(dst_row, 1), :], sem)


def _tile_wait(src_ref, tile_ref, sem):
    rows = tile_ref.shape[0]
    pltpu.make_async_copy(src_ref.at[pl.ds(0, rows), :], tile_ref, sem).wait()


def _dispatch_kernel(p0_ref, p1_ref, gs_ref, gt_ref, x_ref, g_ref, xs_ref, h_ref, sems, *, n_tiles):
    tm = x_ref.shape[0]
    base = pl.program_id(0) * tm

    @pl.when(pl.program_id(0) == 0)
    def _():
        h_ref[...] = jnp.zeros_like(h_ref)

        def zero_tile(tile):
            row = pl.multiple_of(tile * tm, tm)
            return pltpu.make_async_copy(h_ref, xs_ref.at[pl.ds(row, tm), :], sems.at[0])

        n_groups = gs_ref.shape[0]
        for e in range(n_groups):
            @pl.when(gt_ref[e] > 0)
            def _(e=e):
                zero_tile(gs_ref[e] + gt_ref[e] - 1).start()

        used = gs_ref[n_groups - 1] + gt_ref[n_groups - 1]

        def tail(tile, carry):
            zero_tile(tile).start()
            return carry

        lax.fori_loop(used, n_tiles, tail, 0)
        for e in range(n_groups):
            @pl.when(gt_ref[e] > 0)
            def _():
                zero_tile(0).wait()

        def tail_wait(tile, carry):
            zero_tile(0).wait()
            return carry

        lax.fori_loop(used, n_tiles, tail_wait, 0)

    h_ref[...] = _rms(x_ref[...], g_ref[...])

    def start(t, carry):
        _row_copy(h_ref, t, xs_ref, p0_ref[base + t], sems.at[0]).start()
        _row_copy(h_ref, t, xs_ref, p1_ref[base + t], sems.at[1]).start(priority=1)
        return carry

    lax.fori_loop(0, tm, start, 0, unroll=ROW_COPY_UNROLL)
    for k in range(2):
        _tile_wait(xs_ref, h_ref, sems.at[k])


def dispatch(x, g, pos0, pos1, group_start, group_tiles, n_rows):
    t, d = x.shape
    tm = ROUTE_TILE
    grid_spec = pltpu.PrefetchScalarGridSpec(
        num_scalar_prefetch=4,
        grid=(t // tm,),
        in_specs=[pl.BlockSpec((tm, d), lambda i, *_: (i, 0)),
                  pl.BlockSpec((1, d), lambda i, *_: (0, 0))],
        out_specs=pl.BlockSpec(memory_space=pl.ANY),
        scratch_shapes=[pltpu.VMEM((tm, d), F32), pltpu.SemaphoreType.DMA((2,))],
    )
    return pl.pallas_call(
        functools.partial(_dispatch_kernel, n_tiles=n_rows // tm),
        grid_spec=grid_spec,
        out_shape=jax.ShapeDtypeStruct((n_rows, d), F32),
        compiler_params=_cparams(("arbitrary",)),
        name="moe_dispatch",
    )(pos0, pos1, group_start, group_tiles, x, g.reshape(1, d))


def _combine_kernel(p0_ref, p1_ref, x_ref, meta_ref, g_ref, y_ref, o_ref, y0_ref, y1_ref, sems):
    tm = x_ref.shape[0]
    base = pl.program_id(0) * tm

    def start(t, carry):
        _row_copy(y_ref, p0_ref[base + t], y0_ref, t, sems.at[0]).start()
        _row_copy(y_ref, p1_ref[base + t], y1_ref, t, sems.at[1]).start(priority=1)
        return carry

    lax.fori_loop(0, tm, start, 0, unroll=ROW_COPY_UNROLL)
    _tile_wait(y_ref, y0_ref, sems.at[0])
    _tile_wait(y_ref, y1_ref, sems.at[1])
    meta = meta_ref[...]
    g0 = meta[:, R_G0:R_G0 + 1]
    g1 = meta[:, R_G1:R_G1 + 1]
    out = x_ref[...] + g0 * y0_ref[...] + g1 * y1_ref[...]
    o_ref[...] = _rms(out, g_ref[...])


def combine(x, meta, g, y, pos0, pos1):
    t, d = x.shape
    tm = ROUTE_TILE
    grid_spec = pltpu.PrefetchScalarGridSpec(
        num_scalar_prefetch=2,
        grid=(t // tm,),
        in_specs=[pl.BlockSpec((tm, d), lambda i, p0, p1: (i, 0)),
                  pl.BlockSpec((tm, LANES), lambda i, p0, p1: (i, 0)),
                  pl.BlockSpec((1, d), lambda i, p0, p1: (0, 0)),
                  pl.BlockSpec(memory_space=pl.ANY)],
        out_specs=pl.BlockSpec((tm, d), lambda i, p0, p1: (i, 0)),
        scratch_shapes=[pltpu.VMEM((tm, d), F32), pltpu.VMEM((tm, d), F32),
                        pltpu.SemaphoreType.DMA((2,))],
    )
    return pl.pallas_call(
        _combine_kernel,
        grid_spec=grid_spec,
        out_shape=jax.ShapeDtypeStruct((t, d), F32),
        compiler_params=_cparams(("arbitrary",)),
        name="moe_combine",
    )(pos0, pos1, x, meta, g.reshape(1, d), y)


def _moe_plan(meta, counts, tm):
    counts = counts[0, :N_EXPERTS].astype(jnp.int32)
    group_tiles = (counts + tm - 1) // tm
    group_start = jnp.cumsum(group_tiles) - group_tiles
    e0 = meta[:, R_E0].astype(jnp.int32)
    e1 = meta[:, R_E1].astype(jnp.int32)
    pos0 = group_start[e0] * tm + meta[:, R_RANK0].astype(jnp.int32)
    pos1 = group_start[e1] * tm + meta[:, R_RANK1].astype(jnp.int32)
    return pos0, pos1, group_start, group_tiles


def _moe_chunks(group_start, group_tiles, chunk_tiles, n_chunks):
    per = (group_tiles + chunk_tiles - 1) // chunk_tiles
    c_end = jnp.cumsum(per)
    c_start = c_end - per
    total = c_end[-1]
    c = jnp.arange(n_chunks, dtype=jnp.int32)
    c_eff = jnp.minimum(c, jnp.maximum(total - 1, 0))
    expert = jnp.sum((c_eff[:, None] >= c_end[None, :]).astype(jnp.int32), axis=1)
    expert = jnp.minimum(expert, N_EXPERTS - 1)
    k = c_eff - c_start[expert]
    first_tile = group_start[expert] + k * chunk_tiles
    tiles = jnp.clip(group_tiles[expert] - k * chunk_tiles, 0, chunk_tiles)
    tiles = jnp.where(c < total, tiles, 0)
    used = (group_start[-1] + group_tiles[-1]).reshape(1)
    return expert, first_tile, tiles, used


def kernel(x, ev_mix_norm, ev_w_in, ev_na_rpb, ev_da_lambda_q1, ev_da_lambda_k1, ev_da_lambda_q2, ev_da_lambda_k2, ev_da_subln, ev_w_out, ev_ffn_norm, ev_ffn_w_gate, ev_ffn_w_up, ev_ffn_w_down, od_mix_norm, od_w_in, od_conv_w, od_conv_b, od_lru_w_a, od_lru_b_a, od_lru_w_x, od_lru_b_x, od_lru_a_param, od_w_out, od_ffn_norm, od_router, od_moe_w_gate, od_moe_w_up, od_moe_w_down, final_norm):
    b, s, d = x.shape
    t = b * s
    xt = x.reshape(t, d)

    lambda_init = 0.8 - 0.6 * math.exp(-0.3 * 0)
    h = rmsnorm(xt, ev_mix_norm[0], BF16)
    proj = ws_matmul(h, [ev_w_in], tm=512, tn=1024, epilogue="cast", out_dtype=BF16, name="l0_in_proj")
    proj = proj.reshape(b, s, -1)
    na_out = neighbourhood_attention(proj, ev_na_rpb[0])
    lam = (jnp.exp(jnp.sum(ev_da_lambda_q1[0] * ev_da_lambda_k1[0]))
           - jnp.exp(jnp.sum(ev_da_lambda_q2[0] * ev_da_lambda_k2[0])) + lambda_init).reshape(1)
    da_out = differential_attention(proj, lam, ev_da_subln[0], lambda_init)
    mix = jnp.concatenate([na_out, da_out], axis=-1).reshape(t, -1)
    xt = ws_matmul(mix, [ev_w_out], tm=512, tn=1024, epilogue="residual", out_dtype=F32, res=xt,
                   name="l0_out_proj")

    h = rmsnorm(xt, ev_ffn_norm[0], BF16)
    act = ws_matmul(h, [ev_ffn_w_gate, ev_ffn_w_up], tm=512, tn=1024, epilogue="swiglu",
                    out_dtype=BF16, name="l0_ffn_up")
    xt = ws_matmul(act, [ev_ffn_w_down], tm=256, tn=512, epilogue="residual", out_dtype=F32, res=xt,
                   name="l0_ffn_down")

    h = rmsnorm(xt, od_mix_norm[0], BF16)
    proj = ws_matmul(h, [od_w_in], tm=512, tn=1024, epilogue="cast", out_dtype=BF16, name="l1_in_proj")
    rec = recurrent_block(proj.reshape(b, s, -1), od_conv_w[0], od_conv_b[0], od_lru_w_a[0],
                          od_lru_b_a[0], od_lru_w_x[0], od_lru_b_x[0], od_lru_a_param[0])
    xt = ws_matmul(rec.reshape(t, -1), [od_w_out], tm=512, tn=1024, epilogue="residual",
                   out_dtype=F32, res=xt, name="l1_out_proj")

    tm = MOE_ROW_TILE
    n_tiles = (t * 2) // tm + N_EXPERTS
    meta, counts = route(xt, od_ffn_norm[0], od_router[0])
    pos0, pos1, group_start, group_tiles = _moe_plan(meta, counts, tm)
    xs = dispatch(xt, od_ffn_norm[0], pos0, pos1, group_start, group_tiles, n_tiles * tm)
    n_chunks = (n_tiles - N_EXPERTS) // MOE_CHUNK_TILES + N_EXPERTS
    chunks = _moe_chunks(group_start, group_tiles, MOE_CHUNK_TILES, n_chunks)
    y = fused_expert_swiglu(xs, od_moe_w_gate[0], od_moe_w_up[0], od_moe_w_down[0], chunks, tm=tm,
                            chunk_tiles=MOE_CHUNK_TILES, tf=FFN_COL_TILE, name="moe_ffn")
    out = combine(xt, meta, final_norm, y, pos0, pos1)
    return out.reshape(b, s, d)
```

```python
import functools
import math

import numpy as np
import jax
import jax.numpy as jnp
from jax import lax
from jax.experimental import pallas as pl
from jax.experimental.pallas import tpu as pltpu

F32 = jnp.float32
BF16 = jnp.bfloat16

HEAD_DIM = 128
NA_HEADS = 8
DA_HEADS = 8
GRID_W = 64
NA_KH = 8
NA_KW = 16
ROPE_THETA = 10000.0
LRU_BLOCK_DIM = 128
LRU_C = 8.0
N_EXPERTS = 8
EPS = 1e-6
NEG_INF = -1e30
LOG2E = math.log2(math.e)

NA_Q_ROWS = 8
NA_K_ROWS = NA_Q_ROWS + NA_KH
DA_Q_BLOCK = 512
DA_KEY_CHUNK = 256
ROUTE_TILE = 256
MOE_ROW_TILE = ROUTE_TILE
MOE_CHUNK_TILES = 9
FFN_COL_TILE = 256
ROW_COPY_UNROLL = 8
LANES = 128
SUBLANES = 8
VMEM_LIMIT = 56 * 1024 * 1024


def _cparams(semantics):
    return pltpu.CompilerParams(dimension_semantics=semantics, vmem_limit_bytes=VMEM_LIMIT)


def _rms(x, g):
    return x * lax.rsqrt(jnp.mean(x * x, axis=-1, keepdims=True) + EPS) * g


def _rmsnorm_kernel(x_ref, g_ref, o_ref):
    o_ref[...] = _rms(x_ref[...], g_ref[...]).astype(o_ref.dtype)


def rmsnorm(x, g, out_dtype, tm=512):
    t, d = x.shape
    return pl.pallas_call(
        _rmsnorm_kernel,
        grid=(t // tm,),
        in_specs=[pl.BlockSpec((tm, d), lambda i: (i, 0)),
                  pl.BlockSpec((1, d), lambda i: (0, 0))],
        out_specs=pl.BlockSpec((tm, d), lambda i: (i, 0)),
        out_shape=jax.ShapeDtypeStruct((t, d), out_dtype),
        compiler_params=_cparams(("arbitrary",)),
        name="rmsnorm",
    )(x, g.reshape(1, d))


CAST_ROWS = 256


def _cast_weight(w_ref, wbf_ref):
    k = w_ref.shape[0]
    rows = min(CAST_ROWS, k)
    assert k % rows == 0

    def body(c, carry):
        r = pl.multiple_of(c * rows, rows)
        wbf_ref[pl.ds(r, rows), :] = w_ref[pl.ds(r, rows), :].astype(BF16)
        return carry

    lax.fori_loop(0, k // rows, body, 0)


def _mm_kernel(a_ref, *refs, n_w, epilogue):
    w_refs = refs[:n_w]
    pos = n_w
    res_ref = None
    if epilogue == "residual":
        res_ref = refs[pos]
        pos += 1
    o_ref = refs[pos]
    wbf_refs = refs[pos + 1:pos + 1 + n_w]

    @pl.when(pl.program_id(1) == 0)
    def _():
        for w_ref, wbf_ref in zip(w_refs, wbf_refs):
            _cast_weight(w_ref, wbf_ref)

    a = a_ref[...].astype(BF16)
    acc = jnp.dot(a, wbf_refs[0][...], preferred_element_type=F32)
    if epilogue == "swiglu":
        up = jnp.dot(a, wbf_refs[1][...], preferred_element_type=F32)
        acc = acc * jax.nn.sigmoid(acc) * up
    elif epilogue == "residual":
        acc = acc + res_ref[...]
    o_ref[...] = acc.astype(o_ref.dtype)


def ws_matmul(a, ws, *, tm, tn, epilogue, out_dtype, res=None, name):
    m, k = a.shape
    n = ws[0].shape[-1]
    n_w = len(ws)
    in_specs = [pl.BlockSpec((tm, k), lambda j, i: (i, 0))]
    in_specs += [pl.BlockSpec((None, k, tn), lambda j, i: (0, 0, j)) for _ in ws]
    operands = [a, *ws]
    if epilogue == "residual":
        in_specs.append(pl.BlockSpec((tm, tn), lambda j, i: (i, j)))
        operands.append(res)
    return pl.pallas_call(
        functools.partial(_mm_kernel, n_w=n_w, epilogue=epilogue),
        grid=(n // tn, m // tm),
        in_specs=in_specs,
        out_specs=pl.BlockSpec((tm, tn), lambda j, i: (i, j)),
        out_shape=jax.ShapeDtypeStruct((m, n), out_dtype),
        scratch_shapes=[pltpu.VMEM((k, tn), BF16) for _ in ws],
        compiler_params=_cparams(("arbitrary", "arbitrary")),
        name=name,
    )(*operands)


def _silu_mul(g, u):
    return g * jax.nn.sigmoid(g) * u


def _fused_ffn_kernel(ce_ref, cs_ref, cn_ref, used_ref, x_hbm, wg_hbm, wu_hbm, wd_hbm, y_hbm,
                      wg_st, wu_st, wd_st, wg_bf, wu_bf, wd_bf, xbuf, acc, hbuf, stage, pend,
                      w_sem, x_sem, y_sem, *, tm, n_tiles, n_f):
    c = pl.program_id(0)
    last_c = pl.num_programs(0) - 1
    last_f = n_f - 1
    n = cn_ref[c]
    t0 = cs_ref[c]
    tf = wd_st.shape[1]

    def w_copies(chunk, fcol, slot):
        e = ce_ref[chunk]
        col = pl.multiple_of(fcol * tf, tf)
        return (
            (pltpu.make_async_copy(wg_hbm.at[e, :, pl.ds(col, tf)], wg_st.at[slot], w_sem.at[0, slot]), 0),
            (pltpu.make_async_copy(wu_hbm.at[e, :, pl.ds(col, tf)], wu_st.at[slot], w_sem.at[1, slot]), 1),
            (pltpu.make_async_copy(wd_hbm.at[e, pl.ds(col, tf), :], wd_st.at[slot], w_sem.at[2, slot]), 0),
        )

    def w_start(chunk, fcol, slot):
        for copy, priority in w_copies(chunk, fcol, slot):
            copy.start(priority=priority)

    def x_copy(t, slot):
        row = pl.multiple_of((t0 + t) * tm, tm)
        return pltpu.make_async_copy(x_hbm.at[pl.ds(row, tm), :], stage.at[slot], x_sem.at[slot])

    def y_copy(tile, t):
        row = pl.multiple_of(tile * tm, tm)
        return pltpu.make_async_copy(acc.at[t], y_hbm.at[pl.ds(row, tm), :], y_sem)

    def drain_y():
        def wait(t, carry):
            y_copy(0, 0).wait()
            return carry

        lax.fori_loop(0, pend[0], wait, 0)
        pend[0] = 0

    @pl.when(c == 0)
    def _():
        pend[0] = 0

    def up(t):
        a = xbuf[t]
        g = jnp.dot(a, wg_bf[...], preferred_element_type=F32)
        u = jnp.dot(a, wu_bf[...], preferred_element_type=F32)
        return _silu_mul(g, u).astype(BF16)

    def down(t, h):
        acc[t] = acc[t] + jnp.dot(h, wd_bf[...], preferred_element_type=F32)

    def hidden_step(f, carry):
        slot = lax.rem(f, 2)
        nxt = jnp.minimum(c + 1, last_c)

        @pl.when(f < last_f)
        def _():
            w_start(c, f + 1, 1 - slot)

        @pl.when((f == last_f) & (c < last_c) & (cn_ref[nxt] > 0))
        def _():
            w_start(nxt, 0, 0)

        for copy, _ in w_copies(c, f, slot):
            copy.wait()
        for w_st, wbf_ref in ((wg_st, wg_bf), (wu_st, wu_bf), (wd_st, wd_bf)):
            _cast_weight(w_st.at[slot], wbf_ref)

        hbuf[0] = up(0)
        odd = lax.rem(n - 1, 2)

        @pl.when(odd == 1)
        def _():
            h = up(1)
            down(0, hbuf[0])
            hbuf[1] = h

        def two(i, carry):
            t = 1 + odd + 2 * i
            h_prev = hbuf[lax.rem(t - 1, 2)]
            h0 = up(t)
            down(t - 1, h_prev)
            h1 = up(t + 1)
            down(t, h0)
            hbuf[lax.rem(t + 1, 2)] = h1
            return carry

        lax.fori_loop(0, (n - 1) // 2, two, 0)
        down(n - 1, hbuf[lax.rem(n - 1, 2)])
        return carry

    @pl.when(n > 0)
    def _():
        x_copy(0, 0).start()

        @pl.when(c == 0)
        def _():
            w_start(c, 0, 0)

        drain_y()

        def load(t, carry):
            slot = lax.rem(t, 2)

            @pl.when(t + 1 < n)
            def _():
                x_copy(t + 1, 1 - slot).start()

            x_copy(t, slot).wait()
            xbuf[t] = stage[slot].astype(BF16)
            acc[t] = jnp.zeros(acc.shape[1:], F32)
            return carry

        lax.fori_loop(0, n, load, 0)
        lax.fori_loop(0, n_f, hidden_step, 0)

        def store(t, carry):
            y_copy(t0 + t, t).start()
            return carry

        lax.fori_loop(0, n, store, 0)
        pend[0] = n

    @pl.when(c == last_c)
    def _():
        drain_y()
        acc[0] = jnp.zeros(acc.shape[1:], F32)

        def fill(tile, carry):
            y_copy(tile, 0).start()
            y_copy(tile, 0).wait()
            return carry

        lax.fori_loop(used_ref[0], n_tiles, fill, 0)


def fused_expert_swiglu(x, wg, wu, wd, chunks, *, tm, chunk_tiles, tf, name):
    m, k = x.shape
    _, _, d_ff = wg.shape
    d_out = wd.shape[-1]
    n_chunks = chunks[0].shape[0]
    n_f = d_ff // tf

    assert n_f % 2 == 0
    grid_spec = pltpu.PrefetchScalarGridSpec(
        num_scalar_prefetch=4,
        grid=(n_chunks,),
        in_specs=[pl.BlockSpec(memory_space=pl.ANY) for _ in range(4)],
        out_specs=pl.BlockSpec(memory_space=pl.ANY),
        scratch_shapes=[
            pltpu.VMEM((2, k, tf), F32), pltpu.VMEM((2, k, tf), F32), pltpu.VMEM((2, tf, d_out), F32),
            pltpu.VMEM((k, tf), BF16), pltpu.VMEM((k, tf), BF16), pltpu.VMEM((tf, d_out), BF16),
            pltpu.VMEM((chunk_tiles, tm, k), BF16),
            pltpu.VMEM((chunk_tiles, tm, d_out), F32),
            pltpu.VMEM((2, tm, tf), BF16),
            pltpu.VMEM((2, tm, k), x.dtype),
            pltpu.SMEM((1,), jnp.int32),
            pltpu.SemaphoreType.DMA((3, 2)),
            pltpu.SemaphoreType.DMA((2,)),
            pltpu.SemaphoreType.DMA(()),
        ],
    )
    return pl.pallas_call(
        functools.partial(_fused_ffn_kernel, tm=tm, n_tiles=m // tm, n_f=n_f),
        grid_spec=grid_spec,
        out_shape=jax.ShapeDtypeStruct((m, d_out), F32),
        compiler_params=_cparams(("arbitrary",)),
        name=name,
    )(*chunks, x, wg, wu, wd)


def _na_bias_tiles(rpb, rows):
    n_heads, _, n_co = rpb.shape
    c = np.arange(GRID_W)
    c_start = np.clip(c - NA_KW // 2, 0, GRID_W - NA_KW)
    col_in = (c[None, :] >= c_start[:, None]) & (c[None, :] < c_start[:, None] + NA_KW)
    col_off = np.clip(c[None, :] - c[:, None], -(NA_KW - 1), NA_KW - 1) + NA_KW - 1
    pick = (col_off[None] == np.arange(n_co)[:, None, None]).astype(np.float32)
    blocks = jnp.einsum("hrj,jqk->hrqk", rpb.astype(F32) * LOG2E, pick,
                        precision=lax.Precision.HIGHEST)
    blocks = jnp.where(col_in[None, None], blocks, NEG_INF)
    masked = jnp.full((n_heads, GRID_W, GRID_W), NEG_INF, F32)
    tiles = []
    for r0, ks in ((0, 0), (NA_Q_ROWS, NA_Q_ROWS - NA_KH // 2), (rows - NA_Q_ROWS, rows - NA_K_ROWS)):
        q_rows = []
        for qrow in range(r0, r0 + NA_Q_ROWS):
            r_start = min(max(qrow - NA_KH // 2, 0), rows - NA_KH)
            q_rows.append(jnp.concatenate(
                [blocks[:, krow - qrow + NA_KH - 1] if r_start <= krow < r_start + NA_KH else masked
                 for krow in range(ks, ks + NA_K_ROWS)], axis=-1))
        tiles.append(jnp.concatenate(q_rows, axis=1))
    return jnp.stack(tiles, axis=1)


def _na_kernel(q_ref, k_ref, v_ref, b_ref, o_ref, *, rows):
    rb = pl.program_id(2)
    ks = jnp.clip(rb * NA_Q_ROWS - NA_KH // 2, 0, rows - NA_K_ROWS)
    k0 = pl.multiple_of(ks * GRID_W, GRID_W)
    nk = NA_K_ROWS * GRID_W
    kb = k_ref[pl.ds(k0, nk), :]
    vb = v_ref[pl.ds(k0, nk), :]
    s = lax.dot_general(q_ref[...], kb, (((1,), (1,)), ((), ())), preferred_element_type=F32)
    s = s * (HEAD_DIM ** -0.5 * LOG2E) + b_ref[...]
    p = jnp.exp2(s - jnp.max(s, axis=-1, keepdims=True)).astype(BF16)
    v1 = jnp.concatenate([vb, jnp.ones_like(vb)], axis=-1)
    o = jnp.dot(p, v1, preferred_element_type=F32)
    o_ref[...] = (o[:, :HEAD_DIM] / o[:, HEAD_DIM:]).astype(o_ref.dtype)


def neighbourhood_attention(proj, rpb):
    b, s, _ = proj.shape
    rows = s // GRID_W
    n_rb = rows // NA_Q_ROWS
    tq = NA_Q_ROWS * GRID_W
    bias = _na_bias_tiles(rpb, rows)

    def pat(r):
        return jnp.where(r == 0, 0, jnp.where(r == n_rb - 1, 2, 1))

    return pl.pallas_call(
        functools.partial(_na_kernel, rows=rows),
        grid=(b, NA_HEADS, n_rb),
        in_specs=[
            pl.BlockSpec((None, tq, HEAD_DIM), lambda bi, h, r: (bi, r, h)),
            pl.BlockSpec((None, s, HEAD_DIM), lambda bi, h, r: (bi, 0, NA_HEADS + h)),
            pl.BlockSpec((None, s, HEAD_DIM), lambda bi, h, r: (bi, 0, 2 * NA_HEADS + h)),
            pl.BlockSpec((None, None, tq, NA_K_ROWS * GRID_W), lambda bi, h, r: (h, pat(r), 0, 0)),
        ],
        out_specs=pl.BlockSpec((None, tq, HEAD_DIM), lambda bi, h, r: (bi, r, h)),
        out_shape=jax.ShapeDtypeStruct((b, s, NA_HEADS * HEAD_DIM), BF16),
        compiler_params=_cparams(("arbitrary", "arbitrary", "arbitrary")),
        name="na_attention",
    )(proj, proj, proj, bias)


def _rope_tables(seq):
    dq = HEAD_DIM // 2
    inv = 1.0 / (ROPE_THETA ** (jnp.arange(0, dq, 2, dtype=F32) / dq))
    ang = jnp.arange(seq, dtype=F32)[:, None] * inv[None, :]
    cos, sin = jnp.cos(ang), jnp.sin(ang)
    zero = jnp.zeros_like(sin)
    c = jnp.concatenate([cos, cos, cos, cos], axis=-1)
    s_first = jnp.concatenate([-sin, zero, -sin, zero], axis=-1)
    s_second = jnp.concatenate([zero, sin, zero, sin], axis=-1)
    return c, s_first, s_second


def _rope(x, c, s_first, s_second):
    half = HEAD_DIM // 4
    return (x * c + pltpu.roll(x, HEAD_DIM - half, axis=1) * s_first
            + pltpu.roll(x, half, axis=1) * s_second)


def _da_kernel(lam_ref, q_ref, k_ref, v_ref, cq_ref, s1q_ref, s2q_ref, ck_ref, s1k_ref, s2k_ref,
               g_ref, o_ref, kr_ref, v1_ref, *, out_scale):
    s = k_ref.shape[0]

    @pl.when(pl.program_id(2) == 0)
    def _():
        k = k_ref[...].astype(F32)
        kr_ref[...] = _rope(k, ck_ref[...], s1k_ref[...], s2k_ref[...]).astype(BF16)
        v1_ref[:, :HEAD_DIM] = v_ref[...]
        v1_ref[:, HEAD_DIM:] = jnp.ones_like(v_ref)

    dq = HEAD_DIM // 2
    q = _rope(q_ref[...].astype(F32), cq_ref[...], s1q_ref[...], s2q_ref[...]) * (dq ** -0.5 * LOG2E)
    lane = lax.broadcasted_iota(jnp.int32, q.shape, 1)
    dn = (((1,), (1,)), ((), ()))
    kc = min(DA_KEY_CHUNK, s)

    def softmax_av(qm):
        m = jnp.full((qm.shape[0], 1), -jnp.inf, F32)
        acc = jnp.zeros((qm.shape[0], 2 * HEAD_DIM), F32)
        for c0 in range(0, s, kc):
            sc = lax.dot_general(qm, kr_ref[c0:c0 + kc, :], dn, preferred_element_type=F32)
            m_new = jnp.maximum(m, jnp.max(sc, axis=-1, keepdims=True))
            p = jnp.exp2(sc - m_new).astype(BF16)
            acc = jnp.exp2(m - m_new) * acc + jnp.dot(p, v1_ref[c0:c0 + kc, :],
                                                       preferred_element_type=F32)
            m = m_new
        return acc[:, :HEAD_DIM] / acc[:, HEAD_DIM:]

    o = (softmax_av(jnp.where(lane < dq, q, 0.0).astype(BF16))
         - lam_ref[0] * softmax_av(jnp.where(lane >= dq, q, 0.0).astype(BF16)))
    o_ref[...] = (_rms(o, g_ref[...]) * out_scale).astype(o_ref.dtype)


def differential_attention(proj, lam, subln, lambda_init):
    b, s, _ = proj.shape
    tq = min(DA_Q_BLOCK, s)
    base = 3 * NA_HEADS
    c, s_first, s_second = _rope_tables(s)
    qtab = pl.BlockSpec((tq, HEAD_DIM), lambda bi, h, i: (i, 0))
    ktab = pl.BlockSpec((s, HEAD_DIM), lambda bi, h, i: (0, 0))
    return pl.pallas_call(
        functools.partial(_da_kernel, out_scale=1.0 - lambda_init),
        grid=(b, DA_HEADS, s // tq),
        in_specs=[
            pl.BlockSpec(memory_space=pltpu.SMEM),
            pl.BlockSpec((None, tq, HEAD_DIM), lambda bi, h, i: (bi, i, base + h)),
            pl.BlockSpec((None, s, HEAD_DIM), lambda bi, h, i: (bi, 0, base + DA_HEADS + h)),
            pl.BlockSpec((None, s, HEAD_DIM), lambda bi, h, i: (bi, 0, base + 2 * DA_HEADS + h)),
            qtab, qtab, qtab, ktab, ktab, ktab,
            pl.BlockSpec((1, HEAD_DIM), lambda bi, h, i: (0, 0)),
        ],
        out_specs=pl.BlockSpec((None, tq, HEAD_DIM), lambda bi, h, i: (bi, i, h)),
        out_shape=jax.ShapeDtypeStruct((b, s, DA_HEADS * HEAD_DIM), BF16),
        scratch_shapes=[pltpu.VMEM((s, HEAD_DIM), BF16), pltpu.VMEM((s, 2 * HEAD_DIM), BF16)],
        compiler_params=_cparams(("arbitrary", "arbitrary", "arbitrary")),
        name="da_attention",
    )(lam, proj, proj, proj, c, s_first, s_second, c, s_first, s_second, subln.reshape(1, HEAD_DIM))


SCAN_PAD_ROWS = 8


def _shift_rows(x, k):
    return pltpu.roll(x, k % x.shape[0], axis=0)


def _lru_kernel(gate_ref, rec_ref, cw_ref, cb_ref, wa_ref, ba_ref, wx_ref, bx_ref, ap_ref, o_ref,
                u_ref, af_ref, bf_ref, ab_ref, bb_ref):
    s = rec_ref.shape[0]
    seg = s // SUBLANES
    pitch = seg + SCAN_PAD_ROWS

    r = rec_ref[...].astype(F32)
    t_idx = lax.broadcasted_iota(jnp.int32, r.shape, 0)
    cw = cw_ref[...]
    u = r * cw[1:2, :] + cb_ref[...]
    u = u + jnp.where(t_idx >= 1, _shift_rows(r, 1), 0.0) * cw[0:1, :]
    u = u + jnp.where(t_idx < s - 1, _shift_rows(r, -1), 0.0) * cw[2:3, :]
    u = u + jnp.where(t_idx < s - 2, _shift_rows(r, -2), 0.0) * cw[3:4, :]
    u_ref[...] = u

    for d, (a_ref, b_ref) in enumerate(((af_ref, bf_ref), (ab_ref, bb_ref))):
        ap = ap_ref[d]
        softplus = jnp.maximum(-ap, 0.0) + jnp.log1p(jnp.exp(-jnp.abs(ap)))
        wa = wa_ref[d].astype(BF16)
        wx = wx_ref[d].astype(BF16)

        def prep(c, carry, d=d, a_ref=a_ref, b_ref=b_ref, softplus=softplus, wa=wa, wx=wx):
            uc = u_ref[pl.ds(pl.multiple_of(c * seg, seg), seg), :]
            ub = uc.astype(BF16)
            ga = jax.nn.sigmoid(jnp.dot(ub, wa, preferred_element_type=F32) + ba_ref[d])
            gx = jax.nn.sigmoid(jnp.dot(ub, wx, preferred_element_type=F32) + bx_ref[d])
            a = jnp.exp(-LRU_C * ga * softplus)
            r0 = pl.multiple_of(c * pitch, SUBLANES)
            a_ref[pl.ds(r0, seg), :] = a
            b_ref[pl.ds(r0, seg), :] = jnp.sqrt(1.0 - a * a) * (gx * uc)
            return carry

        lax.fori_loop(0, SUBLANES, prep, 0)

    def step(i, carry):
        hf, pf, hb, pb = carry
        rows_f = pl.ds(i, SUBLANES, stride=pitch)
        rows_b = pl.ds(seg - 1 - i, SUBLANES, stride=pitch)
        a_f = af_ref[rows_f, :]
        a_b = ab_ref[rows_b, :]
        hf = a_f * hf + bf_ref[rows_f, :]
        hb = a_b * hb + bb_ref[rows_b, :]
        pf = a_f * pf
        pb = a_b * pb
        bf_ref[rows_f, :] = hf
        af_ref[rows_f, :] = pf
        bb_ref[rows_b, :] = hb
        ab_ref[rows_b, :] = pb
        return hf, pf, hb, pb

    zero = jnp.zeros((SUBLANES, LRU_BLOCK_DIM), F32)
    one = jnp.ones((SUBLANES, LRU_BLOCK_DIM), F32)
    hf, pf, hb, pb = lax.fori_loop(0, seg, step, (zero, one, zero, one), unroll=4)

    cin_f = [jnp.zeros((1, LRU_BLOCK_DIM), F32)]
    for c in range(SUBLANES - 1):
        cin_f.append(pf[c:c + 1] * cin_f[-1] + hf[c:c + 1])
    cin_b = [jnp.zeros((1, LRU_BLOCK_DIM), F32)]
    for c in range(SUBLANES - 1, 0, -1):
        cin_b.insert(0, pb[c:c + 1] * cin_b[0] + hb[c:c + 1])

    for c in range(SUBLANES):
        rows = pl.ds(c * pitch, seg)
        h = (bf_ref[rows, :] + af_ref[rows, :] * cin_f[c]) + (bb_ref[rows, :] + ab_ref[rows, :] * cin_b[c])
        y = jax.nn.gelu(gate_ref[pl.ds(c * seg, seg), :].astype(F32), approximate=True)
        o_ref[pl.ds(c * seg, seg), :] = (y * h).astype(o_ref.dtype)


def recurrent_block(proj, conv_w, conv_b, w_a, b_a, w_x, b_x, a_param):
    b, s, w2 = proj.shape
    w = w2 // 2
    nb = w // LRU_BLOCK_DIM
    c = LRU_BLOCK_DIM
    vec = lambda p: p.reshape(2, 1, w)
    return pl.pallas_call(
        _lru_kernel,
        grid=(b, nb),
        in_specs=[
            pl.BlockSpec((None, s, c), lambda bi, n: (bi, 0, n)),
            pl.BlockSpec((None, s, c), lambda bi, n: (bi, 0, nb + n)),
            pl.BlockSpec((conv_w.shape[0], c), lambda bi, n: (0, n)),
            pl.BlockSpec((1, c), lambda bi, n: (0, n)),
            pl.BlockSpec((2, None, c, c), lambda bi, n: (0, n, 0, 0)),
            pl.BlockSpec((2, 1, c), lambda bi, n: (0, 0, n)),
            pl.BlockSpec((2, None, c, c), lambda bi, n: (0, n, 0, 0)),
            pl.BlockSpec((2, 1, c), lambda bi, n: (0, 0, n)),
            pl.BlockSpec((2, 1, c), lambda bi, n: (0, 0, n)),
        ],
        out_specs=pl.BlockSpec((None, s, c), lambda bi, n: (bi, 0, n)),
        out_shape=jax.ShapeDtypeStruct((b, s, w), BF16),
        scratch_shapes=[pltpu.VMEM((s, c), F32)]
        + [pltpu.VMEM((s + SUBLANES * SCAN_PAD_ROWS, c), F32) for _ in range(4)],
        compiler_params=_cparams(("arbitrary", "arbitrary")),
        name="rg_lru",
    )(proj, proj, conv_w, conv_b.reshape(1, w), w_a, vec(b_a), w_x, vec(b_x), vec(a_param))


R_E0, R_E1, R_G0, R_G1, R_RANK0, R_RANK1 = range(6)


def _route_kernel(x_ref, g_ref, r_ref, meta_ref, cnt_ref, carry_ref):
    i = pl.program_id(0)

    @pl.when(i == 0)
    def _():
        carry_ref[...] = jnp.zeros_like(carry_ref)

    h = _rms(x_ref[...], g_ref[...])
    logits = jnp.dot(h, r_ref[...], preferred_element_type=F32, precision=lax.Precision.HIGHEST)
    tm = logits.shape[0]
    lane = lax.broadcasted_iota(jnp.int32, logits.shape, 1).astype(F32)
    logits = jnp.where(lane < N_EXPERTS, logits, -jnp.inf)
    m0 = jnp.max(logits, axis=-1, keepdims=True)
    e0 = jnp.min(jnp.where(logits == m0, lane, float(LANES)), axis=-1, keepdims=True)
    rest = jnp.where(lane == e0, -jnp.inf, logits)
    m1 = jnp.max(rest, axis=-1, keepdims=True)
    e1 = jnp.min(jnp.where(rest == m1, lane, float(LANES)), axis=-1, keepdims=True)
    z = jnp.exp(m1 - m0)
    g0 = 1.0 / (1.0 + z)
    g1 = z / (1.0 + z)

    hit0 = lane == e0
    hit1 = lane == e1
    both = jnp.where(hit0 | hit1, 1.0, 0.0).astype(BF16)
    ri = lax.broadcasted_iota(jnp.int32, (tm, tm), 0)
    ci = lax.broadcasted_iota(jnp.int32, (tm, tm), 1)
    lower = jnp.where(ci < ri, 1.0, 0.0).astype(BF16)
    before = jnp.dot(lower, both, preferred_element_type=F32) + carry_ref[0:1, :]
    rank0 = jnp.sum(jnp.where(hit0, before, 0.0), axis=-1, keepdims=True)
    rank1 = jnp.sum(jnp.where(hit1, before, 0.0), axis=-1, keepdims=True)
    total = carry_ref[0:1, :] + jnp.sum(both.astype(F32), axis=0, keepdims=True)
    carry_ref[...] = jnp.broadcast_to(total, carry_ref.shape)
    cnt_ref[...] = jnp.broadcast_to(total, cnt_ref.shape)

    rec = jnp.zeros(logits.shape, F32)
    for col, val in ((R_E0, e0), (R_E1, e1), (R_G0, g0), (R_G1, g1),
                     (R_RANK0, rank0), (R_RANK1, rank1)):
        rec = jnp.where(lane == col, val, rec)
    meta_ref[...] = rec


def route(x, g, router):
    t, d = x.shape
    tm = ROUTE_TILE
    r_pad = jnp.zeros((d, LANES), F32).at[:, :N_EXPERTS].set(router)
    return pl.pallas_call(
        _route_kernel,
        grid=(t // tm,),
        in_specs=[pl.BlockSpec((tm, d), lambda i: (i, 0)),
                  pl.BlockSpec((1, d), lambda i: (0, 0)),
                  pl.BlockSpec((d, LANES), lambda i: (0, 0))],
        out_specs=[pl.BlockSpec((tm, LANES), lambda i: (i, 0)),
                   pl.BlockSpec((SUBLANES, LANES), lambda i: (0, 0))],
        out_shape=[jax.ShapeDtypeStruct((t, LANES), F32),
                   jax.ShapeDtypeStruct((SUBLANES, LANES), F32)],
        scratch_shapes=[pltpu.VMEM((SUBLANES, LANES), F32)],
        compiler_params=_cparams(("arbitrary",)),
        name="moe_route",
    )(x, g.reshape(1, d), r_pad)


def _row_copy(src_ref, src_row, dst_ref, dst_row, sem):
    return pltpu.make_async_copy(src_ref.at[pl.ds(src_row, 1), :], dst_ref.at[pl.ds(dst_row, 1), :], sem)


def _tile_wait(src_ref, tile_ref, sem):
    rows = tile_ref.shape[0]
    pltpu.make_async_copy(src_ref.at[pl.ds(0, rows), :], tile_ref, sem).wait()


def _dispatch_kernel(p0_ref, p1_ref, gs_ref, gt_ref, x_ref, g_ref, xs_ref, h_ref, sems, *, n_tiles):
    tm = x_ref.shape[0]
    base = pl.program_id(0) * tm

    @pl.when(pl.program_id(0) == 0)
    def _():
        h_ref[...] = jnp.zeros_like(h_ref)

        def zero_tile(tile):
            row = pl.multiple_of(tile * tm, tm)
            return pltpu.make_async_copy(h_ref, xs_ref.at[pl.ds(row, tm), :], sems.at[0])

        n_groups = gs_ref.shape[0]
        for e in range(n_groups):
            @pl.when(gt_ref[e] > 0)
            def _(e=e):
                zero_tile(gs_ref[e] + gt_ref[e] - 1).start()

        used = gs_ref[n_groups - 1] + gt_ref[n_groups - 1]

        def tail(tile, carry):
            zero_tile(tile).start()
            return carry

        lax.fori_loop(used, n_tiles, tail, 0)
        for e in range(n_groups):
            @pl.when(gt_ref[e] > 0)
            def _():
                zero_tile(0).wait()

        def tail_wait(tile, carry):
            zero_tile(0).wait()
            return carry

        lax.fori_loop(used, n_tiles, tail_wait, 0)

    h_ref[...] = _rms(x_ref[...], g_ref[...])

    def start(t, carry):
        _row_copy(h_ref, t, xs_ref, p0_ref[base + t], sems.at[0]).start()
        _row_copy(h_ref, t, xs_ref, p1_ref[base + t], sems.at[1]).start(priority=1)
        return carry

    lax.fori_loop(0, tm, start, 0, unroll=ROW_COPY_UNROLL)
    for k in range(2):
        _tile_wait(xs_ref, h_ref, sems.at[k])


def dispatch(x, g, pos0, pos1, group_start, group_tiles, n_rows):
    t, d = x.shape
    tm = ROUTE_TILE
    grid_spec = pltpu.PrefetchScalarGridSpec(
        num_scalar_prefetch=4,
        grid=(t // tm,),
        in_specs=[pl.BlockSpec((tm, d), lambda i, *_: (i, 0)),
                  pl.BlockSpec((1, d), lambda i, *_: (0, 0))],
        out_specs=pl.BlockSpec(memory_space=pl.ANY),
        scratch_shapes=[pltpu.VMEM((tm, d), F32), pltpu.SemaphoreType.DMA((2,))],
    )
    return pl.pallas_call(
        functools.partial(_dispatch_kernel, n_tiles=n_rows // tm),
        grid_spec=grid_spec,
        out_shape=jax.ShapeDtypeStruct((n_rows, d), F32),
        compiler_params=_cparams(("arbitrary",)),
        name="moe_dispatch",
    )(pos0, pos1, group_start, group_tiles, x, g.reshape(1, d))


def _combine_kernel(p0_ref, p1_ref, x_ref, meta_ref, g_ref, y_ref, o_ref, y0_ref, y1_ref, sems):
    tm = x_ref.shape[0]
    base = pl.program_id(0) * tm

    def start(t, carry):
        _row_copy(y_ref, p0_ref[base + t], y0_ref, t, sems.at[0]).start()
        _row_copy(y_ref, p1_ref[base + t], y1_ref, t, sems.at[1]).start(priority=1)
        return carry

    lax.fori_loop(0, tm, start, 0, unroll=ROW_COPY_UNROLL)
    _tile_wait(y_ref, y0_ref, sems.at[0])
    _tile_wait(y_ref, y1_ref, sems.at[1])
    meta = meta_ref[...]
    g0 = meta[:, R_G0:R_G0 + 1]
    g1 = meta[:, R_G1:R_G1 + 1]
    out = x_ref[...] + g0 * y0_ref[...] + g1 * y1_ref[...]
    o_ref[...] = _rms(out, g_ref[...])


def combine(x, meta, g, y, pos0, pos1):
    t, d = x.shape
    tm = ROUTE_TILE
    grid_spec = pltpu.PrefetchScalarGridSpec(
        num_scalar_prefetch=2,
        grid=(t // tm,),
        in_specs=[pl.BlockSpec((tm, d), lambda i, p0, p1: (i, 0)),
                  pl.BlockSpec((tm, LANES), lambda i, p0, p1: (i, 0)),
                  pl.BlockSpec((1, d), lambda i, p0, p1: (0, 0)),
                  pl.BlockSpec(memory_space=pl.ANY)],
        out_specs=pl.BlockSpec((tm, d), lambda i, p0, p1: (i, 0)),
        scratch_shapes=[pltpu.VMEM((tm, d), F32), pltpu.VMEM((tm, d), F32),
                        pltpu.SemaphoreType.DMA((2,))],
    )
    return pl.pallas_call(
        _combine_kernel,
        grid_spec=grid_spec,
        out_shape=jax.ShapeDtypeStruct((t, d), F32),
        compiler_params=_cparams(("arbitrary",)),
        name="moe_combine",
    )(pos0, pos1, x, meta, g.reshape(1, d), y)


def _moe_plan(meta, counts, tm):
    counts = counts[0, :N_EXPERTS].astype(jnp.int32)
    group_tiles = (counts + tm - 1) // tm
    group_start = jnp.cumsum(group_tiles) - group_tiles
    e0 = meta[:, R_E0].astype(jnp.int32)
    e1 = meta[:, R_E1].astype(jnp.int32)
    pos0 = group_start[e0] * tm + meta[:, R_RANK0].astype(jnp.int32)
    pos1 = group_start[e1] * tm + meta[:, R_RANK1].astype(jnp.int32)
    return pos0, pos1, group_start, group_tiles


def _moe_chunks(group_start, group_tiles, chunk_tiles, n_chunks):
    per = (group_tiles + chunk_tiles - 1) // chunk_tiles
    c_end = jnp.cumsum(per)
    c_start = c_end - per
    total = c_end[-1]
    c = jnp.arange(n_chunks, dtype=jnp.int32)
    c_eff = jnp.minimum(c, jnp.maximum(total - 1, 0))
    expert = jnp.sum((c_eff[:, None] >= c_end[None, :]).astype(jnp.int32), axis=1)
    expert = jnp.minimum(expert, N_EXPERTS - 1)
    k = c_eff - c_start[expert]
    first_tile = group_start[expert] + k * chunk_tiles
    tiles = jnp.clip(group_tiles[expert] - k * chunk_tiles, 0, chunk_tiles)
    tiles = jnp.where(c < total, tiles, 0)
    used = (group_start[-1] + group_tiles[-1]).reshape(1)
    return expert, first_tile, tiles, used


def kernel(x, ev_mix_norm, ev_w_in, ev_na_rpb, ev_da_lambda_q1, ev_da_lambda_k1, ev_da_lambda_q2, ev_da_lambda_k2, ev_da_subln, ev_w_out, ev_ffn_norm, ev_ffn_w_gate, ev_ffn_w_up, ev_ffn_w_down, od_mix_norm, od_w_in, od_conv_w, od_conv_b, od_lru_w_a, od_lru_b_a, od_lru_w_x, od_lru_b_x, od_lru_a_param, od_w_out, od_ffn_norm, od_router, od_moe_w_gate, od_moe_w_up, od_moe_w_down, final_norm):
    b, s, d = x.shape
    t = b * s
    xt = x.reshape(t, d)

    lambda_init = 0.8 - 0.6 * math.exp(-0.3 * 0)
    h = rmsnorm(xt, ev_mix_norm[0], BF16)
    proj = ws_matmul(h, [ev_w_in], tm=512, tn=1024, epilogue="cast", out_dtype=BF16, name="l0_in_proj")
    proj = proj.reshape(b, s, -1)
    na_out = neighbourhood_attention(proj, ev_na_rpb[0])
    lam = (jnp.exp(jnp.sum(ev_da_lambda_q1[0] * ev_da_lambda_k1[0]))
           - jnp.exp(jnp.sum(ev_da_lambda_q2[0] * ev_da_lambda_k2[0])) + lambda_init).reshape(1)
    da_out = differential_attention(proj, lam, ev_da_subln[0], lambda_init)
    mix = jnp.concatenate([na_out, da_out], axis=-1).reshape(t, -1)
    xt = ws_matmul(mix, [ev_w_out], tm=512, tn=1024, epilogue="residual", out_dtype=F32, res=xt,
                   name="l0_out_proj")

    h = rmsnorm(xt, ev_ffn_norm[0], BF16)
    act = ws_matmul(h, [ev_ffn_w_gate, ev_ffn_w_up], tm=512, tn=1024, epilogue="swiglu",
                    out_dtype=BF16, name="l0_ffn_up")
    xt = ws_matmul(act, [ev_ffn_w_down], tm=256, tn=512, epilogue="residual", out_dtype=F32, res=xt,
                   name="l0_ffn_down")

    h = rmsnorm(xt, od_mix_norm[0], BF16)
    proj = ws_matmul(h, [od_w_in], tm=512, tn=1024, epilogue="cast", out_dtype=BF16, name="l1_in_proj")
    rec = recurrent_block(proj.reshape(b, s, -1), od_conv_w[0], od_conv_b[0], od_lru_w_a[0],
                          od_lru_b_a[0], od_lru_w_x[0], od_lru_b_x[0], od_lru_a_param[0])
    xt = ws_matmul(rec.reshape(t, -1), [od_w_out], tm=512, tn=1024, epilogue="residual",
                   out_dtype=F32, res=xt, name="l1_out_proj")

    tm = MOE_ROW_TILE
    n_tiles = (t * 2) // tm + N_EXPERTS
    meta, counts = route(xt, od_ffn_norm[0], od_router[0])
    pos0, pos1, group_start, group_tiles = _moe_plan(meta, counts, tm)
    xs = dispatch(xt, od_ffn_norm[0], pos0, pos1, group_start, group_tiles, n_tiles * tm)
    n_chunks = (n_tiles - N_EXPERTS) // MOE_CHUNK_TILES + N_EXPERTS
    chunks = _moe_chunks(group_start, group_tiles, MOE_CHUNK_TILES, n_chunks)
    y = fused_expert_swiglu(xs, od_moe_w_gate[0], od_moe_w_up[0], od_moe_w_down[0], chunks, tm=tm,
                            chunk_tiles=MOE_CHUNK_TILES, tf=FFN_COL_TILE, name="moe_ffn")
    out = combine(xt, meta, final_norm, y, pos0, pos1)
    return out.reshape(b, s, d)
```

```python
import functools
import math

import numpy as np
import jax
import jax.numpy as jnp
from jax import lax
from jax.experimental import pallas as pl
from jax.experimental.pallas import tpu as pltpu

F32 = jnp.float32
BF16 = jnp.bfloat16

HEAD_DIM = 128
NA_HEADS = 8
DA_HEADS = 8
GRID_W = 64
NA_KH = 8
NA_KW = 16
ROPE_THETA = 10000.0
LRU_BLOCK_DIM = 128
LRU_C = 8.0
N_EXPERTS = 8
EPS = 1e-6
NEG_INF = -1e30
LOG2E = math.log2(math.e)

NA_Q_ROWS = 8
NA_K_ROWS = NA_Q_ROWS + NA_KH
DA_Q_BLOCK = 512
DA_KEY_CHUNK = 256
ROUTE_TILE = 256
MOE_ROW_TILE = ROUTE_TILE
MOE_CHUNK_TILES = 9
FFN_COL_TILE = 256
ROW_COPY_UNROLL = 8
LANES = 128
SUBLANES = 8
VMEM_LIMIT = 56 * 1024 * 1024


def _cparams(semantics):
    return pltpu.CompilerParams(dimension_semantics=semantics, vmem_limit_bytes=VMEM_LIMIT)


def _rms(x, g):
    return x * lax.rsqrt(jnp.mean(x * x, axis=-1, keepdims=True) + EPS) * g


def _sigmoid(x):
    return 0.5 * jnp.tanh(0.5 * x) + 0.5


def _rmsnorm_kernel(x_ref, g_ref, o_ref):
    o_ref[...] = _rms(x_ref[...], g_ref[...]).astype(o_ref.dtype)


def rmsnorm(x, g, out_dtype, tm=512):
    t, d = x.shape
    return pl.pallas_call(
        _rmsnorm_kernel,
        grid=(t // tm,),
        in_specs=[pl.BlockSpec((tm, d), lambda i: (i, 0)),
                  pl.BlockSpec((1, d), lambda i: (0, 0))],
        out_specs=pl.BlockSpec((tm, d), lambda i: (i, 0)),
        out_shape=jax.ShapeDtypeStruct((t, d), out_dtype),
        compiler_params=_cparams(("arbitrary",)),
        name="rmsnorm",
    )(x, g.reshape(1, d))


CAST_ROWS = 256


def _cast_weight(w_ref, wbf_ref):
    k = w_ref.shape[0]
    rows = min(CAST_ROWS, k)
    assert k % rows == 0

    def body(c, carry):
        r = pl.multiple_of(c * rows, rows)
        wbf_ref[pl.ds(r, rows), :] = w_ref[pl.ds(r, rows), :].astype(BF16)
        return carry

    lax.fori_loop(0, k // rows, body, 0)


def _mm_kernel(a_ref, *refs, n_w, epilogue):
    w_refs = refs[:n_w]
    pos = n_w
    res_ref = None
    if epilogue == "residual":
        res_ref = refs[pos]
        pos += 1
    o_ref = refs[pos]
    wbf_refs = refs[pos + 1:pos + 1 + n_w]

    @pl.when(pl.program_id(1) == 0)
    def _():
        for w_ref, wbf_ref in zip(w_refs, wbf_refs):
            _cast_weight(w_ref, wbf_ref)

    a = a_ref[...].astype(BF16)
    acc = jnp.dot(a, wbf_refs[0][...], preferred_element_type=F32)
    if epilogue == "swiglu":
        up = jnp.dot(a, wbf_refs[1][...], preferred_element_type=F32)
        acc = acc * _sigmoid(acc) * up
    elif epilogue == "residual":
        acc = acc + res_ref[...]
    o_ref[...] = acc.astype(o_ref.dtype)


def ws_matmul(a, ws, *, tm, tn, epilogue, out_dtype, res=None, name):
    m, k = a.shape
    n = ws[0].shape[-1]
    n_w = len(ws)
    in_specs = [pl.BlockSpec((tm, k), lambda j, i: (i, 0))]
    in_specs += [pl.BlockSpec((None, k, tn), lambda j, i: (0, 0, j)) for _ in ws]
    operands = [a, *ws]
    if epilogue == "residual":
        in_specs.append(pl.BlockSpec((tm, tn), lambda j, i: (i, j)))
        operands.append(res)
    return pl.pallas_call(
        functools.partial(_mm_kernel, n_w=n_w, epilogue=epilogue),
        grid=(n // tn, m // tm),
        in_specs=in_specs,
        out_specs=pl.BlockSpec((tm, tn), lambda j, i: (i, j)),
        out_shape=jax.ShapeDtypeStruct((m, n), out_dtype),
        scratch_shapes=[pltpu.VMEM((k, tn), BF16) for _ in ws],
        compiler_params=_cparams(("arbitrary", "arbitrary")),
        name=name,
    )(*operands)


def _silu_mul(g, u):
    return g * _sigmoid(g) * u


def _fused_ffn_kernel(ce_ref, cs_ref, cn_ref, used_ref, x_hbm, wg_hbm, wu_hbm, wd_hbm, y_hbm,
                      wg_st, wu_st, wd_st, wg_bf, wu_bf, wd_bf, xbuf, acc, hbuf, stage, pend,
                      w_sem, x_sem, y_sem, *, tm, n_tiles, n_f):
    c = pl.program_id(0)
    last_c = pl.num_programs(0) - 1
    last_f = n_f - 1
    n = cn_ref[c]
    t0 = cs_ref[c]
    tf = wd_st.shape[1]

    def w_copies(chunk, fcol, slot):
        e = ce_ref[chunk]
        col = pl.multiple_of(fcol * tf, tf)
        return (
            (pltpu.make_async_copy(wg_hbm.at[e, :, pl.ds(col, tf)], wg_st.at[slot], w_sem.at[0, slot]), 0),
            (pltpu.make_async_copy(wu_hbm.at[e, :, pl.ds(col, tf)], wu_st.at[slot], w_sem.at[1, slot]), 1),
            (pltpu.make_async_copy(wd_hbm.at[e, pl.ds(col, tf), :], wd_st.at[slot], w_sem.at[2, slot]), 1),
        )

    def w_start(chunk, fcol, slot):
        for copy, priority in w_copies(chunk, fcol, slot):
            copy.start(priority=priority)

    def x_copy(t, slot):
        row = pl.multiple_of((t0 + t) * tm, tm)
        return pltpu.make_async_copy(x_hbm.at[pl.ds(row, tm), :], stage.at[slot], x_sem.at[slot])

    def y_copy(tile, t):
        row = pl.multiple_of(tile * tm, tm)
        return pltpu.make_async_copy(acc.at[t], y_hbm.at[pl.ds(row, tm), :], y_sem)

    def drain_y():
        def wait(t, carry):
            y_copy(0, 0).wait()
            return carry

        lax.fori_loop(0, pend[0], wait, 0)
        pend[0] = 0

    @pl.when(c == 0)
    def _():
        pend[0] = 0

    def up(t):
        a = xbuf[t]
        g = jnp.dot(a, wg_bf[...], preferred_element_type=F32)
        u = jnp.dot(a, wu_bf[...], preferred_element_type=F32)
        return _silu_mul(g, u).astype(BF16)

    def down(t, h):
        acc[t] = acc[t] + jnp.dot(h, wd_bf[...], preferred_element_type=F32)

    def hidden_step(f, carry):
        slot = lax.rem(f, 2)
        nxt = jnp.minimum(c + 1, last_c)

        @pl.when(f < last_f)
        def _():
            w_start(c, f + 1, 1 - slot)

        @pl.when((f == last_f) & (c < last_c) & (cn_ref[nxt] > 0))
        def _():
            w_start(nxt, 0, 0)

        for copy, _ in w_copies(c, f, slot):
            copy.wait()
        for w_st, wbf_ref in ((wg_st, wg_bf), (wu_st, wu_bf), (wd_st, wd_bf)):
            _cast_weight(w_st.at[slot], wbf_ref)

        hbuf[0] = up(0)
        odd = lax.rem(n - 1, 2)

        @pl.when(odd == 1)
        def _():
            h = up(1)
            down(0, hbuf[0])
            hbuf[1] = h

        def two(i, carry):
            t = 1 + odd + 2 * i
            h_prev = hbuf[lax.rem(t - 1, 2)]
            h0 = up(t)
            down(t - 1, h_prev)
            h1 = up(t + 1)
            down(t, h0)
            hbuf[lax.rem(t + 1, 2)] = h1
            return carry

        lax.fori_loop(0, (n - 1) // 2, two, 0)
        down(n - 1, hbuf[lax.rem(n - 1, 2)])
        return carry

    @pl.when(n > 0)
    def _():
        x_copy(0, 0).start()

        @pl.when(c == 0)
        def _():
            w_start(c, 0, 0)

        drain_y()

        def load(t, carry):
            slot = lax.rem(t, 2)

            @pl.when(t + 1 < n)
            def _():
                x_copy(t + 1, 1 - slot).start()

            x_copy(t, slot).wait()
            xbuf[t] = stage[slot].astype(BF16)
            acc[t] = jnp.zeros(acc.shape[1:], F32)
            return carry

        lax.fori_loop(0, n, load, 0)
        lax.fori_loop(0, n_f, hidden_step, 0)

        def store(t, carry):
            y_copy(t0 + t, t).start()
            return carry

        lax.fori_loop(0, n, store, 0)
        pend[0] = n

    @pl.when(c == last_c)
    def _():
        drain_y()
        acc[0] = jnp.zeros(acc.shape[1:], F32)

        def fill(tile, carry):
            y_copy(tile, 0).start()
            y_copy(tile, 0).wait()
            return carry

        lax.fori_loop(used_ref[0], n_tiles, fill, 0)


def fused_expert_swiglu(x, wg, wu, wd, chunks, *, tm, chunk_tiles, tf, name):
    m, k = x.shape
    _, _, d_ff = wg.shape
    d_out = wd.shape[-1]
    n_chunks = chunks[0].shape[0]
    n_f = d_ff // tf

    assert n_f % 2 == 0
    grid_spec = pltpu.PrefetchScalarGridSpec(
        num_scalar_prefetch=4,
        grid=(n_chunks,),
        in_specs=[pl.BlockSpec(memory_space=pl.ANY) for _ in range(4)],
        out_specs=pl.BlockSpec(memory_space=pl.ANY),
        scratch_shapes=[
            pltpu.VMEM((2, k, tf), F32), pltpu.VMEM((2, k, tf), F32), pltpu.VMEM((2, tf, d_out), F32),
            pltpu.VMEM((k, tf), BF16), pltpu.VMEM((k, tf), BF16), pltpu.VMEM((tf, d_out), BF16),
            pltpu.VMEM((chunk_tiles, tm, k), BF16),
            pltpu.VMEM((chunk_tiles, tm, d_out), F32),
            pltpu.VMEM((2, tm, tf), BF16),
            pltpu.VMEM((2, tm, k), x.dtype),
            pltpu.SMEM((1,), jnp.int32),
            pltpu.SemaphoreType.DMA((3, 2)),
            pltpu.SemaphoreType.DMA((2,)),
            pltpu.SemaphoreType.DMA(()),
        ],
    )
    return pl.pallas_call(
        functools.partial(_fused_ffn_kernel, tm=tm, n_tiles=m // tm, n_f=n_f),
        grid_spec=grid_spec,
        out_shape=jax.ShapeDtypeStruct((m, d_out), F32),
        compiler_params=_cparams(("arbitrary",)),
        name=name,
    )(*chunks, x, wg, wu, wd)


def _na_bias_tiles(rpb, rows):
    n_heads, _, n_co = rpb.shape
    c = np.arange(GRID_W)
    c_start = np.clip(c - NA_KW // 2, 0, GRID_W - NA_KW)
    col_in = (c[None, :] >= c_start[:, None]) & (c[None, :] < c_start[:, None] + NA_KW)
    col_off = np.clip(c[None, :] - c[:, None], -(NA_KW - 1), NA_KW - 1) + NA_KW - 1
    pick = (col_off[None] == np.arange(n_co)[:, None, None]).astype(np.float32)
    blocks = jnp.einsum("hrj,jqk->hrqk", rpb.astype(F32) * LOG2E, pick,
                        precision=lax.Precision.HIGHEST)
    blocks = jnp.where(col_in[None, None], blocks, NEG_INF)
    masked = jnp.full((n_heads, GRID_W, GRID_W), NEG_INF, F32)
    tiles = []
    for r0, ks in ((0, 0), (NA_Q_ROWS, NA_Q_ROWS - NA_KH // 2), (rows - NA_Q_ROWS, rows - NA_K_ROWS)):
        q_rows = []
        for qrow in range(r0, r0 + NA_Q_ROWS):
            r_start = min(max(qrow - NA_KH // 2, 0), rows - NA_KH)
            q_rows.append(jnp.concatenate(
                [blocks[:, krow - qrow + NA_KH - 1] if r_start <= krow < r_start + NA_KH else masked
                 for krow in range(ks, ks + NA_K_ROWS)], axis=-1))
        tiles.append(jnp.concatenate(q_rows, axis=1))
    return jnp.stack(tiles, axis=1)


def _na_kernel(q_ref, k_ref, v_ref, b_ref, o_ref, *, rows):
    rb = pl.program_id(2)
    ks = jnp.clip(rb * NA_Q_ROWS - NA_KH // 2, 0, rows - NA_K_ROWS)
    k0 = pl.multiple_of(ks * GRID_W, GRID_W)
    nk = NA_K_ROWS * GRID_W
    kb = k_ref[pl.ds(k0, nk), :]
    vb = v_ref[pl.ds(k0, nk), :]
    s = lax.dot_general(q_ref[...], kb, (((1,), (1,)), ((), ())), preferred_element_type=F32)
    s = s * (HEAD_DIM ** -0.5 * LOG2E) + b_ref[...]
    p = jnp.exp2(s - jnp.max(s, axis=-1, keepdims=True)).astype(BF16)
    v1 = jnp.concatenate([vb, jnp.ones_like(vb)], axis=-1)
    o = jnp.dot(p, v1, preferred_element_type=F32)
    o_ref[...] = (o[:, :HEAD_DIM] / o[:, HEAD_DIM:]).astype(o_ref.dtype)


def neighbourhood_attention(proj, rpb):
    b, s, _ = proj.shape
    rows = s // GRID_W
    n_rb = rows // NA_Q_ROWS
    tq = NA_Q_ROWS * GRID_W
    bias = _na_bias_tiles(rpb, rows)

    def pat(r):
        return jnp.where(r == 0, 0, jnp.where(r == n_rb - 1, 2, 1))

    return pl.pallas_call(
        functools.partial(_na_kernel, rows=rows),
        grid=(b, NA_HEADS, n_rb),
        in_specs=[
            pl.BlockSpec((None, tq, HEAD_DIM), lambda bi, h, r: (bi, r, h)),
            pl.BlockSpec((None, s, HEAD_DIM), lambda bi, h, r: (bi, 0, NA_HEADS + h)),
            pl.BlockSpec((None, s, HEAD_DIM), lambda bi, h, r: (bi, 0, 2 * NA_HEADS + h)),
            pl.BlockSpec((None, None, tq, NA_K_ROWS * GRID_W), lambda bi, h, r: (h, pat(r), 0, 0)),
        ],
        out_specs=pl.BlockSpec((None, tq, HEAD_DIM), lambda bi, h, r: (bi, r, h)),
        out_shape=jax.ShapeDtypeStruct((b, s, NA_HEADS * HEAD_DIM), BF16),
        compiler_params=_cparams(("arbitrary", "arbitrary", "arbitrary")),
        name="na_attention",
    )(proj, proj, proj, bias)


def _rope_tables(seq):
    dq = HEAD_DIM // 2
    inv = 1.0 / (ROPE_THETA ** (jnp.arange(0, dq, 2, dtype=F32) / dq))
    ang = jnp.arange(seq, dtype=F32)[:, None] * inv[None, :]
    cos, sin = jnp.cos(ang), jnp.sin(ang)
    zero = jnp.zeros_like(sin)
    c = jnp.concatenate([cos, cos, cos, cos], axis=-1)
    s_first = jnp.concatenate([-sin, zero, -sin, zero], axis=-1)
    s_second = jnp.concatenate([zero, sin, zero, sin], axis=-1)
    return c, s_first, s_second


def _rope(x, c, s_first, s_second):
    half = HEAD_DIM // 4
    return (x * c + pltpu.roll(x, HEAD_DIM - half, axis=1) * s_first
            + pltpu.roll(x, half, axis=1) * s_second)


def _da_kernel(lam_ref, q_ref, k_ref, v_ref, cq_ref, s1q_ref, s2q_ref, ck_ref, s1k_ref, s2k_ref,
               g_ref, o_ref, kr_ref, v1_ref, *, out_scale):
    s = k_ref.shape[0]

    @pl.when(pl.program_id(2) == 0)
    def _():
        k = k_ref[...].astype(F32)
        kr_ref[...] = _rope(k, ck_ref[...], s1k_ref[...], s2k_ref[...]).astype(BF16)
        v1_ref[:, :HEAD_DIM] = v_ref[...]
        v1_ref[:, HEAD_DIM:] = jnp.ones_like(v_ref)

    dq = HEAD_DIM // 2
    q = _rope(q_ref[...].astype(F32), cq_ref[...], s1q_ref[...], s2q_ref[...]) * (dq ** -0.5 * LOG2E)
    lane = lax.broadcasted_iota(jnp.int32, q.shape, 1)
    dn = (((1,), (1,)), ((), ()))
    kc = min(DA_KEY_CHUNK, s)

    def softmax_av(qm):
        m = jnp.full((qm.shape[0], 1), -jnp.inf, F32)
        acc = jnp.zeros((qm.shape[0], 2 * HEAD_DIM), F32)
        for c0 in range(0, s, kc):
            sc = lax.dot_general(qm, kr_ref[c0:c0 + kc, :], dn, preferred_element_type=F32)
            m_new = jnp.maximum(m, jnp.max(sc, axis=-1, keepdims=True))
            p = jnp.exp2(sc - m_new).astype(BF16)
            acc = jnp.exp2(m - m_new) * acc + jnp.dot(p, v1_ref[c0:c0 + kc, :],
                                                       preferred_element_type=F32)
            m = m_new
        return acc[:, :HEAD_DIM] / acc[:, HEAD_DIM:]

    o = (softmax_av(jnp.where(lane < dq, q, 0.0).astype(BF16))
         - lam_ref[0] * softmax_av(jnp.where(lane >= dq, q, 0.0).astype(BF16)))
    o_ref[...] = (_rms(o, g_ref[...]) * out_scale).astype(o_ref.dtype)


def differential_attention(proj, lam, subln, lambda_init):
    b, s, _ = proj.shape
    tq = min(DA_Q_BLOCK, s)
    base = 3 * NA_HEADS
    c, s_first, s_second = _rope_tables(s)
    qtab = pl.BlockSpec((tq, HEAD_DIM), lambda bi, h, i: (i, 0))
    ktab = pl.BlockSpec((s, HEAD_DIM), lambda bi, h, i: (0, 0))
    return pl.pallas_call(
        functools.partial(_da_kernel, out_scale=1.0 - lambda_init),
        grid=(b, DA_HEADS, s // tq),
        in_specs=[
            pl.BlockSpec(memory_space=pltpu.SMEM),
            pl.BlockSpec((None, tq, HEAD_DIM), lambda bi, h, i: (bi, i, base + h)),
            pl.BlockSpec((None, s, HEAD_DIM), lambda bi, h, i: (bi, 0, base + DA_HEADS + h)),
            pl.BlockSpec((None, s, HEAD_DIM), lambda bi, h, i: (bi, 0, base + 2 * DA_HEADS + h)),
            qtab, qtab, qtab, ktab, ktab, ktab,
            pl.BlockSpec((1, HEAD_DIM), lambda bi, h, i: (0, 0)),
        ],
        out_specs=pl.BlockSpec((None, tq, HEAD_DIM), lambda bi, h, i: (bi, i, h)),
        out_shape=jax.ShapeDtypeStruct((b, s, DA_HEADS * HEAD_DIM), BF16),
        scratch_shapes=[pltpu.VMEM((s, HEAD_DIM), BF16), pltpu.VMEM((s, 2 * HEAD_DIM), BF16)],
        compiler_params=_cparams(("arbitrary", "arbitrary", "arbitrary")),
        name="da_attention",
    )(lam, proj, proj, proj, c, s_first, s_second, c, s_first, s_second, subln.reshape(1, HEAD_DIM))


SCAN_PAD_ROWS = 8


def _shift_rows(x, k):
    return pltpu.roll(x, k % x.shape[0], axis=0)


def _lru_kernel(gate_ref, rec_ref, cw_ref, cb_ref, wa_ref, ba_ref, wx_ref, bx_ref, ap_ref, o_ref,
                u_ref, af_ref, bf_ref, ab_ref, bb_ref):
    s = rec_ref.shape[0]
    seg = s // SUBLANES
    pitch = seg + SCAN_PAD_ROWS

    r = rec_ref[...].astype(F32)
    t_idx = lax.broadcasted_iota(jnp.int32, r.shape, 0)
    cw = cw_ref[...]
    u = r * cw[1:2, :] + cb_ref[...]
    u = u + jnp.where(t_idx >= 1, _shift_rows(r, 1), 0.0) * cw[0:1, :]
    u = u + jnp.where(t_idx < s - 1, _shift_rows(r, -1), 0.0) * cw[2:3, :]
    u = u + jnp.where(t_idx < s - 2, _shift_rows(r, -2), 0.0) * cw[3:4, :]
    u_ref[...] = u

    for d, (a_ref, b_ref) in enumerate(((af_ref, bf_ref), (ab_ref, bb_ref))):
        ap = ap_ref[d]
        softplus = jnp.maximum(-ap, 0.0) + jnp.log1p(jnp.exp(-jnp.abs(ap)))
        wa = wa_ref[d].astype(BF16)
        wx = wx_ref[d].astype(BF16)

        def prep(c, carry, d=d, a_ref=a_ref, b_ref=b_ref, softplus=softplus, wa=wa, wx=wx):
            uc = u_ref[pl.ds(pl.multiple_of(c * seg, seg), seg), :]
            ub = uc.astype(BF16)
            ga = _sigmoid(jnp.dot(ub, wa, preferred_element_type=F32) + ba_ref[d])
            gx = _sigmoid(jnp.dot(ub, wx, preferred_element_type=F32) + bx_ref[d])
            a = jnp.exp(-LRU_C * ga * softplus)
            r0 = pl.multiple_of(c * pitch, SUBLANES)
            a_ref[pl.ds(r0, seg), :] = a
            b_ref[pl.ds(r0, seg), :] = jnp.sqrt(1.0 - a * a) * (gx * uc)
            return carry

        lax.fori_loop(0, SUBLANES, prep, 0)

    def step(i, carry):
        hf, pf, hb, pb = carry
        rows_f = pl.ds(i, SUBLANES, stride=pitch)
        rows_b = pl.ds(seg - 1 - i, SUBLANES, stride=pitch)
        a_f = af_ref[rows_f, :]
        a_b = ab_ref[rows_b, :]
        hf = a_f * hf + bf_ref[rows_f, :]
        hb = a_b * hb + bb_ref[rows_b, :]
        pf = a_f * pf
        pb = a_b * pb
        bf_ref[rows_f, :] = hf
        af_ref[rows_f, :] = pf
        bb_ref[rows_b, :] = hb
        ab_ref[rows_b, :] = pb
        return hf, pf, hb, pb

    zero = jnp.zeros((SUBLANES, LRU_BLOCK_DIM), F32)
    one = jnp.ones((SUBLANES, LRU_BLOCK_DIM), F32)
    hf, pf, hb, pb = lax.fori_loop(0, seg, step, (zero, one, zero, one), unroll=4)

    cin_f = [jnp.zeros((1, LRU_BLOCK_DIM), F32)]
    for c in range(SUBLANES - 1):
        cin_f.append(pf[c:c + 1] * cin_f[-1] + hf[c:c + 1])
    cin_b = [jnp.zeros((1, LRU_BLOCK_DIM), F32)]
    for c in range(SUBLANES - 1, 0, -1):
        cin_b.insert(0, pb[c:c + 1] * cin_b[0] + hb[c:c + 1])

    for c in range(SUBLANES):
        rows = pl.ds(c * pitch, seg)
        h = (bf_ref[rows, :] + af_ref[rows, :] * cin_f[c]) + (bb_ref[rows, :] + ab_ref[rows, :] * cin_b[c])
        y = jax.nn.gelu(gate_ref[pl.ds(c * seg, seg), :].astype(F32), approximate=True)
        o_ref[pl.ds(c * seg, seg), :] = (y * h).astype(o_ref.dtype)


def recurrent_block(proj, conv_w, conv_b, w_a, b_a, w_x, b_x, a_param):
    b, s, w2 = proj.shape
    w = w2 // 2
    nb = w // LRU_BLOCK_DIM
    c = LRU_BLOCK_DIM
    vec = lambda p: p.reshape(2, 1, w)
    return pl.pallas_call(
        _lru_kernel,
        grid=(b, nb),
        in_specs=[
            pl.BlockSpec((None, s, c), lambda bi, n: (bi, 0, n)),
            pl.BlockSpec((None, s, c), lambda bi, n: (bi, 0, nb + n)),
            pl.BlockSpec((conv_w.shape[0], c), lambda bi, n: (0, n)),
            pl.BlockSpec((1, c), lambda bi, n: (0, n)),
            pl.BlockSpec((2, None, c, c), lambda bi, n: (0, n, 0, 0)),
            pl.BlockSpec((2, 1, c), lambda bi, n: (0, 0, n)),
            pl.BlockSpec((2, None, c, c), lambda bi, n: (0, n, 0, 0)),
            pl.BlockSpec((2, 1, c), lambda bi, n: (0, 0, n)),
            pl.BlockSpec((2, 1, c), lambda bi, n: (0, 0, n)),
        ],
        out_specs=pl.BlockSpec((None, s, c), lambda bi, n: (bi, 0, n)),
        out_shape=jax.ShapeDtypeStruct((b, s, w), BF16),
        scratch_shapes=[pltpu.VMEM((s, c), F32)]
        + [pltpu.VMEM((s + SUBLANES * SCAN_PAD_ROWS, c), F32) for _ in range(4)],
        compiler_params=_cparams(("arbitrary", "arbitrary")),
        name="rg_lru",
    )(proj, proj, conv_w, conv_b.reshape(1, w), w_a, vec(b_a), w_x, vec(b_x), vec(a_param))


R_E0, R_E1, R_G0, R_G1, R_RANK0, R_RANK1 = range(6)


def _route_kernel(x_ref, g_ref, r_ref, meta_ref, cnt_ref, carry_ref):
    i = pl.program_id(0)

    @pl.when(i == 0)
    def _():
        carry_ref[...] = jnp.zeros_like(carry_ref)

    h = _rms(x_ref[...], g_ref[...])
    r = r_ref[...]
    h_hi = h.astype(BF16)
    r_hi = r.astype(BF16)
    h_lo = (h - h_hi.astype(F32)).astype(BF16)
    r_lo = (r - r_hi.astype(F32)).astype(BF16)
    logits = (jnp.dot(h_hi, r_hi, preferred_element_type=F32)
              + jnp.dot(h_lo, r_hi, preferred_element_type=F32)
              + jnp.dot(h_hi, r_lo, preferred_element_type=F32))
    tm = logits.shape[0]
    lane = lax.broadcasted_iota(jnp.int32, logits.shape, 1).astype(F32)
    logits = jnp.where(lane < N_EXPERTS, logits, -jnp.inf)
    m0 = jnp.max(logits, axis=-1, keepdims=True)
    e0 = jnp.min(jnp.where(logits == m0, lane, float(LANES)), axis=-1, keepdims=True)
    rest = jnp.where(lane == e0, -jnp.inf, logits)
    m1 = jnp.max(rest, axis=-1, keepdims=True)
    e1 = jnp.min(jnp.where(rest == m1, lane, float(LANES)), axis=-1, keepdims=True)
    z = jnp.exp(m1 - m0)
    g0 = 1.0 / (1.0 + z)
    g1 = z / (1.0 + z)

    hit0 = lane == e0
    hit1 = lane == e1
    both = jnp.where(hit0 | hit1, 1.0, 0.0).astype(BF16)
    ri = lax.broadcasted_iota(jnp.int32, (tm, tm), 0)
    ci = lax.broadcasted_iota(jnp.int32, (tm, tm), 1)
    lower = jnp.where(ci < ri, 1.0, 0.0).astype(BF16)
    before = jnp.dot(lower, both, preferred_element_type=F32) + carry_ref[0:1, :]
    rank0 = jnp.sum(jnp.where(hit0, before, 0.0), axis=-1, keepdims=True)
    rank1 = jnp.sum(jnp.where(hit1, before, 0.0), axis=-1, keepdims=True)
    total = carry_ref[0:1, :] + jnp.sum(both.astype(F32), axis=0, keepdims=True)
    carry_ref[...] = jnp.broadcast_to(total, carry_ref.shape)
    cnt_ref[...] = jnp.broadcast_to(total, cnt_ref.shape)

    rec = jnp.zeros(logits.shape, F32)
    for col, val in ((R_E0, e0), (R_E1, e1), (R_G0, g0), (R_G1, g1),
                     (R_RANK0, rank0), (R_RANK1, rank1)):
        rec = jnp.where(lane == col, val, rec)
    meta_ref[...] = rec


def route(x, g, router):
    t, d = x.shape
    tm = ROUTE_TILE
    r_pad = jnp.zeros((d, LANES), F32).at[:, :N_EXPERTS].set(router)
    return pl.pallas_call(
        _route_kernel,
        grid=(t // tm,),
        in_specs=[pl.BlockSpec((tm, d), lambda i: (i, 0)),
                  pl.BlockSpec((1, d), lambda i: (0, 0)),
                  pl.BlockSpec((d, LANES), lambda i: (0, 0))],
        out_specs=[pl.BlockSpec((tm, LANES), lambda i: (i, 0)),
                   pl.BlockSpec((SUBLANES, LANES), lambda i: (0, 0))],
        out_shape=[jax.ShapeDtypeStruct((t, LANES), F32),
                   jax.ShapeDtypeStruct((SUBLANES, LANES), F32)],
        scratch_shapes=[pltpu.VMEM((SUBLANES, LANES), F32)],
        compiler_params=_cparams(("arbitrary",)),
        name="moe_route",
    )(x, g.reshape(1, d), r_pad)


def _row_copy(src_ref, src_row, dst_ref, dst_row, sem):
    return pltpu.make_async_copy(src_ref.at[pl.ds(src_row, 1), :], dst_ref.at[pl.ds(dst_row, 1), :], sem)


def _tile_wait(src_ref, tile_ref, sem):
    rows = tile_ref.shape[0]
    pltpu.make_async_copy(src_ref.at[pl.ds(0, rows), :], tile_ref, sem).wait()


def _dispatch_kernel(p0_ref, p1_ref, gs_ref, gt_ref, x_ref, g_ref, xs_ref, h_ref, sems, *, n_tiles):
    tm = x_ref.shape[0]
    base = pl.program_id(0) * tm

    @pl.when(pl.program_id(0) == 0)
    def _():
        h_ref[...] = jnp.zeros_like(h_ref)

        def zero_tile(tile):
            row = pl.multiple_of(tile * tm, tm)
            return pltpu.make_async_copy(h_ref, xs_ref.at[pl.ds(row, tm), :], sems.at[0])

        n_groups = gs_ref.shape[0]
        for e in range(n_groups):
            @pl.when(gt_ref[e] > 0)
            def _(e=e):
                zero_tile(gs_ref[e] + gt_ref[e] - 1).start()

        used = gs_ref[n_groups - 1] + gt_ref[n_groups - 1]

        def tail(tile, carry):
            zero_tile(tile).start()
            return carry

        lax.fori_loop(used, n_tiles, tail, 0)
        for e in range(n_groups):
            @pl.when(gt_ref[e] > 0)
            def _():
                zero_tile(0).wait()

        def tail_wait(tile, carry):
            zero_tile(0).wait()
            return carry

        lax.fori_loop(used, n_tiles, tail_wait, 0)

    h_ref[...] = _rms(x_ref[...], g_ref[...])

    def start(t, carry):
        _row_copy(h_ref, t, xs_ref, p0_ref[base + t], sems.at[0]).start()
        _row_copy(h_ref, t, xs_ref, p1_ref[base + t], sems.at[1]).start(priority=1)
        return carry

    lax.fori_loop(0, tm, start, 0, unroll=ROW_COPY_UNROLL)
    for k in range(2):
        _tile_wait(xs_ref, h_ref, sems.at[k])


def dispatch(x, g, pos0, pos1, group_start, group_tiles, n_rows):
    t, d = x.shape
    tm = ROUTE_TILE
    grid_spec = pltpu.PrefetchScalarGridSpec(
        num_scalar_prefetch=4,
        grid=(t // tm,),
        in_specs=[pl.BlockSpec((tm, d), lambda i, *_: (i, 0)),
                  pl.BlockSpec((1, d), lambda i, *_: (0, 0))],
        out_specs=pl.BlockSpec(memory_space=pl.ANY),
        scratch_shapes=[pltpu.VMEM((tm, d), F32), pltpu.SemaphoreType.DMA((2,))],
    )
    return pl.pallas_call(
        functools.partial(_dispatch_kernel, n_tiles=n_rows // tm),
        grid_spec=grid_spec,
        out_shape=jax.ShapeDtypeStruct((n_rows, d), F32),
        compiler_params=_cparams(("arbitrary",)),
        name="moe_dispatch",
    )(pos0, pos1, group_start, group_tiles, x, g.reshape(1, d))


def _combine_kernel(p0_ref, p1_ref, x_ref, meta_ref, g_ref, y_ref, o_ref, y0_ref, y1_ref, sems):
    tm = x_ref.shape[0]
    base = pl.program_id(0) * tm

    def start(t, carry):
        _row_copy(y_ref, p0_ref[base + t], y0_ref, t, sems.at[0]).start()
        _row_copy(y_ref, p1_ref[base + t], y1_ref, t, sems.at[1]).start(priority=1)
        return carry

    lax.fori_loop(0, tm, start, 0, unroll=ROW_COPY_UNROLL)
    _tile_wait(y_ref, y0_ref, sems.at[0])
    _tile_wait(y_ref, y1_ref, sems.at[1])
    meta = meta_ref[...]
    g0 = meta[:, R_G0:R_G0 + 1]
    g1 = meta[:, R_G1:R_G1 + 1]
    out = x_ref[...] + g0 * y0_ref[...] + g1 * y1_ref[...]
    o_ref[...] = _rms(out, g_ref[...])


def combine(x, meta, g, y, pos0, pos1):
    t, d = x.shape
    tm = ROUTE_TILE
    grid_spec = pltpu.PrefetchScalarGridSpec(
        num_scalar_prefetch=2,
        grid=(t // tm,),
        in_specs=[pl.BlockSpec((tm, d), lambda i, p0, p1: (i, 0)),
                  pl.BlockSpec((tm, LANES), lambda i, p0, p1: (i, 0)),
                  pl.BlockSpec((1, d), lambda i, p0, p1: (0, 0)),
                  pl.BlockSpec(memory_space=pl.ANY)],
        out_specs=pl.BlockSpec((tm, d), lambda i, p0, p1: (i, 0)),
        scratch_shapes=[pltpu.VMEM((tm, d), F32), pltpu.VMEM((tm, d), F32),
                        pltpu.SemaphoreType.DMA((2,))],
    )
    return pl.pallas_call(
        _combine_kernel,
        grid_spec=grid_spec,
        out_shape=jax.ShapeDtypeStruct((t, d), F32),
        compiler_params=_cparams(("arbitrary",)),
        name="moe_combine",
    )(pos0, pos1, x, meta, g.reshape(1, d), y)


def _moe_plan(meta, counts, tm):
    counts = counts[0, :N_EXPERTS].astype(jnp.int32)
    group_tiles = (counts + tm - 1) // tm
    group_start = jnp.cumsum(group_tiles) - group_tiles
    e0 = meta[:, R_E0].astype(jnp.int32)
    e1 = meta[:, R_E1].astype(jnp.int32)
    pos0 = group_start[e0] * tm + meta[:, R_RANK0].astype(jnp.int32)
    pos1 = group_start[e1] * tm + meta[:, R_RANK1].astype(jnp.int32)
    return pos0, pos1, group_start, group_tiles


def _moe_chunks(group_start, group_tiles, chunk_tiles, n_chunks):
    per = (group_tiles + chunk_tiles - 1) // chunk_tiles
    c_end = jnp.cumsum(per)
    c_start = c_end - per
    total = c_end[-1]
    c = jnp.arange(n_chunks, dtype=jnp.int32)
    c_eff = jnp.minimum(c, jnp.maximum(total - 1, 0))
    expert = jnp.sum((c_eff[:, None] >= c_end[None, :]).astype(jnp.int32), axis=1)
    expert = jnp.minimum(expert, N_EXPERTS - 1)
    k = c_eff - c_start[expert]
    first_tile = group_start[expert] + k * chunk_tiles
    tiles = jnp.clip(group_tiles[expert] - k * chunk_tiles, 0, chunk_tiles)
    tiles = jnp.where(c < total, tiles, 0)
    used = (group_start[-1] + group_tiles[-1]).reshape(1)
    return expert, first_tile, tiles, used


def kernel(x, ev_mix_norm, ev_w_in, ev_na_rpb, ev_da_lambda_q1, ev_da_lambda_k1, ev_da_lambda_q2, ev_da_lambda_k2, ev_da_subln, ev_w_out, ev_ffn_norm, ev_ffn_w_gate, ev_ffn_w_up, ev_ffn_w_down, od_mix_norm, od_w_in, od_conv_w, od_conv_b, od_lru_w_a, od_lru_b_a, od_lru_w_x, od_lru_b_x, od_lru_a_param, od_w_out, od_ffn_norm, od_router, od_moe_w_gate, od_moe_w_up, od_moe_w_down, final_norm):
    b, s, d = x.shape
    t = b * s
    xt = x.reshape(t, d)

    lambda_init = 0.8 - 0.6 * math.exp(-0.3 * 0)
    h = rmsnorm(xt, ev_mix_norm[0], BF16)
    proj = ws_matmul(h, [ev_w_in], tm=512, tn=1024, epilogue="cast", out_dtype=BF16, name="l0_in_proj")
    proj = proj.reshape(b, s, -1)
    na_out = neighbourhood_attention(proj, ev_na_rpb[0])
    lam = (jnp.exp(jnp.sum(ev_da_lambda_q1[0] * ev_da_lambda_k1[0]))
           - jnp.exp(jnp.sum(ev_da_lambda_q2[0] * ev_da_lambda_k2[0])) + lambda_init).reshape(1)
    da_out = differential_attention(proj, lam, ev_da_subln[0], lambda_init)
    mix = jnp.concatenate([na_out, da_out], axis=-1).reshape(t, -1)
    xt = ws_matmul(mix, [ev_w_out], tm=512, tn=1024, epilogue="residual", out_dtype=F32, res=xt,
                   name="l0_out_proj")

    h = rmsnorm(xt, ev_ffn_norm[0], BF16)
    act = ws_matmul(h, [ev_ffn_w_gate, ev_ffn_w_up], tm=512, tn=1024, epilogue="swiglu",
                    out_dtype=BF16, name="l0_ffn_up")
    xt = ws_matmul(act, [ev_ffn_w_down], tm=256, tn=512, epilogue="residual", out_dtype=F32, res=xt,
                   name="l0_ffn_down")

    h = rmsnorm(xt, od_mix_norm[0], BF16)
    proj = ws_matmul(h, [od_w_in], tm=512, tn=1024, epilogue="cast", out_dtype=BF16, name="l1_in_proj")
    rec = recurrent_block(proj.reshape(b, s, -1), od_conv_w[0], od_conv_b[0], od_lru_w_a[0],
                          od_lru_b_a[0], od_lru_w_x[0], od_lru_b_x[0], od_lru_a_param[0])
    xt = ws_matmul(rec.reshape(t, -1), [od_w_out], tm=512, tn=1024, epilogue="residual",
                   out_dtype=F32, res=xt, name="l1_out_proj")

    tm = MOE_ROW_TILE
    n_tiles = (t * 2) // tm + N_EXPERTS
    meta, counts = route(xt, od_ffn_norm[0], od_router[0])
    pos0, pos1, group_start, group_tiles = _moe_plan(meta, counts, tm)
    xs = dispatch(xt, od_ffn_norm[0], pos0, pos1, group_start, group_tiles, n_tiles * tm)
    n_chunks = (n_tiles - N_EXPERTS) // MOE_CHUNK_TILES + N_EXPERTS
    chunks = _moe_chunks(group_start, group_tiles, MOE_CHUNK_TILES, n_chunks)
    y = fused_expert_swiglu(xs, od_moe_w_gate[0], od_moe_w_up[0], od_moe_w_down[0], chunks, tm=tm,
                            chunk_tiles=MOE_CHUNK_TILES, tf=FFN_COL_TILE, name="moe_ffn")
    out = combine(xt, meta, final_norm, y, pos0, pos1)
    return out.reshape(b, s, d)
```

```python
import functools
import math

import numpy as np
import jax
import jax.numpy as jnp
from jax import lax
from jax.experimental import pallas as pl
from jax.experimental.pallas import tpu as pltpu

F32 = jnp.float32
BF16 = jnp.bfloat16

HEAD_DIM = 128
NA_HEADS = 8
DA_HEADS = 8
GRID_W = 64
NA_KH = 8
NA_KW = 16
ROPE_THETA = 10000.0
LRU_BLOCK_DIM = 128
LRU_C = 8.0
N_EXPERTS = 8
EPS = 1e-6
NEG_INF = -1e30
LOG2E = math.log2(math.e)

NA_Q_ROWS = 8
NA_K_ROWS = NA_Q_ROWS + NA_KH
DA_Q_BLOCK = 1024
DA_KEY_CHUNK = 256
ROUTE_TILE = 256
MOE_ROW_TILE = ROUTE_TILE
MOE_CHUNK_TILES = 9
FFN_COL_TILE = 256
ROW_COPY_UNROLL = 8
LANES = 128
SUBLANES = 8
VMEM_LIMIT = 56 * 1024 * 1024


def _cparams(semantics):
    return pltpu.CompilerParams(dimension_semantics=semantics, vmem_limit_bytes=VMEM_LIMIT)


def _rms(x, g):
    return x * lax.rsqrt(jnp.mean(x * x, axis=-1, keepdims=True) + EPS) * g


def _sigmoid(x):
    return 0.5 * jnp.tanh(0.5 * x) + 0.5


def _rmsnorm_kernel(x_ref, g_ref, o_ref):
    o_ref[...] = _rms(x_ref[...], g_ref[...]).astype(o_ref.dtype)


def rmsnorm(x, g, out_dtype, tm=512):
    t, d = x.shape
    return pl.pallas_call(
        _rmsnorm_kernel,
        grid=(t // tm,),
        in_specs=[pl.BlockSpec((tm, d), lambda i: (i, 0)),
                  pl.BlockSpec((1, d), lambda i: (0, 0))],
        out_specs=pl.BlockSpec((tm, d), lambda i: (i, 0)),
        out_shape=jax.ShapeDtypeStruct((t, d), out_dtype),
        compiler_params=_cparams(("arbitrary",)),
        name="rmsnorm",
    )(x, g.reshape(1, d))


CAST_ROWS = 256


def _cast_weight(w_ref, wbf_ref):
    k = w_ref.shape[0]
    rows = min(CAST_ROWS, k)
    assert k % rows == 0

    def body(c, carry):
        r = pl.multiple_of(c * rows, rows)
        wbf_ref[pl.ds(r, rows), :] = w_ref[pl.ds(r, rows), :].astype(BF16)
        return carry

    lax.fori_loop(0, k // rows, body, 0)


def _mm_kernel(a_ref, *refs, n_w, epilogue):
    w_refs = refs[:n_w]
    pos = n_w
    res_ref = None
    if epilogue == "residual":
        res_ref = refs[pos]
        pos += 1
    o_ref = refs[pos]
    wbf_refs = refs[pos + 1:pos + 1 + n_w]

    @pl.when(pl.program_id(1) == 0)
    def _():
        for w_ref, wbf_ref in zip(w_refs, wbf_refs):
            _cast_weight(w_ref, wbf_ref)

    a = a_ref[...].astype(BF16)
    acc = jnp.dot(a, wbf_refs[0][...], preferred_element_type=F32)
    if epilogue == "swiglu":
        up = jnp.dot(a, wbf_refs[1][...], preferred_element_type=F32)
        acc = acc * _sigmoid(acc) * up
    elif epilogue == "residual":
        acc = acc + res_ref[...]
    o_ref[...] = acc.astype(o_ref.dtype)


def ws_matmul(a, ws, *, tm, tn, epilogue, out_dtype, res=None, name):
    m, k = a.shape
    n = ws[0].shape[-1]
    n_w = len(ws)
    in_specs = [pl.BlockSpec((tm, k), lambda j, i: (i, 0))]
    in_specs += [pl.BlockSpec((None, k, tn), lambda j, i: (0, 0, j)) for _ in ws]
    operands = [a, *ws]
    if epilogue == "residual":
        in_specs.append(pl.BlockSpec((tm, tn), lambda j, i: (i, j)))
        operands.append(res)
    return pl.pallas_call(
        functools.partial(_mm_kernel, n_w=n_w, epilogue=epilogue),
        grid=(n // tn, m // tm),
        in_specs=in_specs,
        out_specs=pl.BlockSpec((tm, tn), lambda j, i: (i, j)),
        out_shape=jax.ShapeDtypeStruct((m, n), out_dtype),
        scratch_shapes=[pltpu.VMEM((k, tn), BF16) for _ in ws],
        compiler_params=_cparams(("arbitrary", "arbitrary")),
        name=name,
    )(*operands)


def _silu_mul(g, u):
    return g * _sigmoid(g) * u


def _fused_ffn_kernel(ce_ref, cs_ref, cn_ref, used_ref, x_hbm, wg_hbm, wu_hbm, wd_hbm, y_hbm,
                      wg_st, wu_st, wd_st, wg_bf, wu_bf, wd_bf, xbuf, acc, hbuf, stage, pend,
                      w_sem, x_sem, y_sem, *, tm, n_tiles, n_f):
    c = pl.program_id(0)
    last_c = pl.num_programs(0) - 1
    last_f = n_f - 1
    n = cn_ref[c]
    t0 = cs_ref[c]
    tf = wd_st.shape[1]

    def w_copies(chunk, fcol, slot):
        e = ce_ref[chunk]
        col = pl.multiple_of(fcol * tf, tf)
        return (
            (pltpu.make_async_copy(wg_hbm.at[e, :, pl.ds(col, tf)], wg_st.at[slot], w_sem.at[0, slot]), 0),
            (pltpu.make_async_copy(wu_hbm.at[e, :, pl.ds(col, tf)], wu_st.at[slot], w_sem.at[1, slot]), 1),
            (pltpu.make_async_copy(wd_hbm.at[e, pl.ds(col, tf), :], wd_st.at[slot], w_sem.at[2, slot]), 1),
        )

    def w_start(chunk, fcol, slot):
        for copy, priority in w_copies(chunk, fcol, slot):
            copy.start(priority=priority)

    def x_copy(t, slot):
        row = pl.multiple_of((t0 + t) * tm, tm)
        return pltpu.make_async_copy(x_hbm.at[pl.ds(row, tm), :], stage.at[slot], x_sem.at[slot])

    def y_copy(tile, t):
        row = pl.multiple_of(tile * tm, tm)
        return pltpu.make_async_copy(acc.at[t], y_hbm.at[pl.ds(row, tm), :], y_sem)

    def drain_y():
        def wait(t, carry):
            y_copy(0, 0).wait()
            return carry

        lax.fori_loop(0, pend[0], wait, 0)
        pend[0] = 0

    @pl.when(c == 0)
    def _():
        pend[0] = 0

    def up(t):
        a = xbuf[t]
        g = jnp.dot(a, wg_bf[...], preferred_element_type=F32)
        u = jnp.dot(a, wu_bf[...], preferred_element_type=F32)
        return _silu_mul(g, u).astype(BF16)

    def down(t, h):
        acc[t] = acc[t] + jnp.dot(h, wd_bf[...], preferred_element_type=F32)

    def hidden_step(f, carry):
        slot = lax.rem(f, 2)
        nxt = jnp.minimum(c + 1, last_c)

        @pl.when(f < last_f)
        def _():
            w_start(c, f + 1, 1 - slot)

        @pl.when((f == last_f) & (c < last_c) & (cn_ref[nxt] > 0))
        def _():
            w_start(nxt, 0, 0)

        for copy, _ in w_copies(c, f, slot):
            copy.wait()
        for w_st, wbf_ref in ((wg_st, wg_bf), (wu_st, wu_bf), (wd_st, wd_bf)):
            _cast_weight(w_st.at[slot], wbf_ref)

        hbuf[0] = up(0)
        odd = lax.rem(n - 1, 2)

        @pl.when(odd == 1)
        def _():
            h = up(1)
            down(0, hbuf[0])
            hbuf[1] = h

        def two(i, carry):
            t = 1 + odd + 2 * i
            h_prev = hbuf[lax.rem(t - 1, 2)]
            h0 = up(t)
            down(t - 1, h_prev)
            h1 = up(t + 1)
            down(t, h0)
            hbuf[lax.rem(t + 1, 2)] = h1
            return carry

        lax.fori_loop(0, (n - 1) // 2, two, 0)
        down(n - 1, hbuf[lax.rem(n - 1, 2)])
        return carry

    @pl.when(n > 0)
    def _():
        x_copy(0, 0).start()

        @pl.when(c == 0)
        def _():
            w_start(c, 0, 0)

        drain_y()

        def load(t, carry):
            slot = lax.rem(t, 2)

            @pl.when(t + 1 < n)
            def _():
                x_copy(t + 1, 1 - slot).start()

            x_copy(t, slot).wait()
            xbuf[t] = stage[slot].astype(BF16)
            acc[t] = jnp.zeros(acc.shape[1:], F32)
            return carry

        lax.fori_loop(0, n, load, 0)
        lax.fori_loop(0, n_f, hidden_step, 0)

        def store(t, carry):
            y_copy(t0 + t, t).start()
            return carry

        lax.fori_loop(0, n, store, 0)
        pend[0] = n

    @pl.when(c == last_c)
    def _():
        drain_y()
        acc[0] = jnp.zeros(acc.shape[1:], F32)

        def fill(tile, carry):
            y_copy(tile, 0).start()
            y_copy(tile, 0).wait()
            return carry

        lax.fori_loop(used_ref[0], n_tiles, fill, 0)


def fused_expert_swiglu(x, wg, wu, wd, chunks, *, tm, chunk_tiles, tf, name):
    m, k = x.shape
    _, _, d_ff = wg.shape
    d_out = wd.shape[-1]
    n_chunks = chunks[0].shape[0]
    n_f = d_ff // tf

    assert n_f % 2 == 0
    grid_spec = pltpu.PrefetchScalarGridSpec(
        num_scalar_prefetch=4,
        grid=(n_chunks,),
        in_specs=[pl.BlockSpec(memory_space=pl.ANY) for _ in range(4)],
        out_specs=pl.BlockSpec(memory_space=pl.ANY),
        scratch_shapes=[
            pltpu.VMEM((2, k, tf), F32), pltpu.VMEM((2, k, tf), F32), pltpu.VMEM((2, tf, d_out), F32),
            pltpu.VMEM((k, tf), BF16), pltpu.VMEM((k, tf), BF16), pltpu.VMEM((tf, d_out), BF16),
            pltpu.VMEM((chunk_tiles, tm, k), BF16),
            pltpu.VMEM((chunk_tiles, tm, d_out), F32),
            pltpu.VMEM((2, tm, tf), BF16),
            pltpu.VMEM((2, tm, k), x.dtype),
            pltpu.SMEM((1,), jnp.int32),
            pltpu.SemaphoreType.DMA((3, 2)),
            pltpu.SemaphoreType.DMA((2,)),
            pltpu.SemaphoreType.DMA(()),
        ],
    )
    return pl.pallas_call(
        functools.partial(_fused_ffn_kernel, tm=tm, n_tiles=m // tm, n_f=n_f),
        grid_spec=grid_spec,
        out_shape=jax.ShapeDtypeStruct((m, d_out), F32),
        compiler_params=_cparams(("arbitrary",)),
        name=name,
    )(*chunks, x, wg, wu, wd)


def _na_bias_tiles(rpb, rows):
    n_heads, _, n_co = rpb.shape
    c = np.arange(GRID_W)
    c_start = np.clip(c - NA_KW // 2, 0, GRID_W - NA_KW)
    col_in = (c[None, :] >= c_start[:, None]) & (c[None, :] < c_start[:, None] + NA_KW)
    col_off = np.clip(c[None, :] - c[:, None], -(NA_KW - 1), NA_KW - 1) + NA_KW - 1
    pick = (col_off[None] == np.arange(n_co)[:, None, None]).astype(np.float32)
    blocks = jnp.einsum("hrj,jqk->hrqk", rpb.astype(F32) * LOG2E, pick,
                        precision=lax.Precision.HIGHEST)
    blocks = jnp.where(col_in[None, None], blocks, NEG_INF)
    masked = jnp.full((n_heads, GRID_W, GRID_W), NEG_INF, F32)
    tiles = []
    for r0, ks in ((0, 0), (NA_Q_ROWS, NA_Q_ROWS - NA_KH // 2), (rows - NA_Q_ROWS, rows - NA_K_ROWS)):
        q_rows = []
        for qrow in range(r0, r0 + NA_Q_ROWS):
            r_start = min(max(qrow - NA_KH // 2, 0), rows - NA_KH)
            q_rows.append(jnp.concatenate(
                [blocks[:, krow - qrow + NA_KH - 1] if r_start <= krow < r_start + NA_KH else masked
                 for krow in range(ks, ks + NA_K_ROWS)], axis=-1))
        tiles.append(jnp.concatenate(q_rows, axis=1))
    return jnp.stack(tiles, axis=1)


def _na_kernel(q_ref, k_ref, v_ref, b_ref, o_ref, *, rows):
    rb = pl.program_id(2)
    ks = jnp.clip(rb * NA_Q_ROWS - NA_KH // 2, 0, rows - NA_K_ROWS)
    k0 = pl.multiple_of(ks * GRID_W, GRID_W)
    nk = NA_K_ROWS * GRID_W
    kb = k_ref[pl.ds(k0, nk), :]
    vb = v_ref[pl.ds(k0, nk), :]
    s = lax.dot_general(q_ref[...], kb, (((1,), (1,)), ((), ())), preferred_element_type=F32)
    s = s * (HEAD_DIM ** -0.5 * LOG2E) + b_ref[...]
    p = jnp.exp2(s - jnp.max(s, axis=-1, keepdims=True)).astype(BF16)
    v1 = jnp.concatenate([vb, jnp.ones_like(vb)], axis=-1)
    o = jnp.dot(p, v1, preferred_element_type=F32)
    o_ref[...] = (o[:, :HEAD_DIM] / o[:, HEAD_DIM:]).astype(o_ref.dtype)


def neighbourhood_attention(proj, rpb):
    b, s, _ = proj.shape
    rows = s // GRID_W
    n_rb = rows // NA_Q_ROWS
    tq = NA_Q_ROWS * GRID_W
    bias = _na_bias_tiles(rpb, rows)

    def pat(r):
        return jnp.where(r == 0, 0, jnp.where(r == n_rb - 1, 2, 1))

    return pl.pallas_call(
        functools.partial(_na_kernel, rows=rows),
        grid=(b, NA_HEADS, n_rb),
        in_specs=[
            pl.BlockSpec((None, tq, HEAD_DIM), lambda bi, h, r: (bi, r, h)),
            pl.BlockSpec((None, s, HEAD_DIM), lambda bi, h, r: (bi, 0, NA_HEADS + h)),
            pl.BlockSpec((None, s, HEAD_DIM), lambda bi, h, r: (bi, 0, 2 * NA_HEADS + h)),
            pl.BlockSpec((None, None, tq, NA_K_ROWS * GRID_W), lambda bi, h, r: (h, pat(r), 0, 0)),
        ],
        out_specs=pl.BlockSpec((None, tq, HEAD_DIM), lambda bi, h, r: (bi, r, h)),
        out_shape=jax.ShapeDtypeStruct((b, s, NA_HEADS * HEAD_DIM), BF16),
        compiler_params=_cparams(("arbitrary", "arbitrary", "arbitrary")),
        name="na_attention",
    )(proj, proj, proj, bias)


def _rope_tables(seq):
    dq = HEAD_DIM // 2
    inv = 1.0 / (ROPE_THETA ** (jnp.arange(0, dq, 2, dtype=F32) / dq))
    ang = jnp.arange(seq, dtype=F32)[:, None] * inv[None, :]
    cos, sin = jnp.cos(ang), jnp.sin(ang)
    zero = jnp.zeros_like(sin)
    c = jnp.concatenate([cos, cos, cos, cos], axis=-1)
    s_first = jnp.concatenate([-sin, zero, -sin, zero], axis=-1)
    s_second = jnp.concatenate([zero, sin, zero, sin], axis=-1)
    return c, s_first, s_second


def _rope(x, c, s_first, s_second):
    half = HEAD_DIM // 4
    return (x * c + pltpu.roll(x, HEAD_DIM - half, axis=1) * s_first
            + pltpu.roll(x, half, axis=1) * s_second)


def _da_kernel(lam_ref, q_ref, k_ref, v_ref, cq_ref, s1q_ref, s2q_ref, ck_ref, s1k_ref, s2k_ref,
               g_ref, o_ref, kr_ref, v1_ref, *, out_scale):
    s = k_ref.shape[0]

    @pl.when(pl.program_id(2) == 0)
    def _():
        k = k_ref[...].astype(F32)
        kr_ref[...] = _rope(k, ck_ref[...], s1k_ref[...], s2k_ref[...]).astype(BF16)
        v1_ref[:, :HEAD_DIM] = v_ref[...]
        v1_ref[:, HEAD_DIM:] = jnp.ones_like(v_ref)

    dq = HEAD_DIM // 2
    q = _rope(q_ref[...].astype(F32), cq_ref[...], s1q_ref[...], s2q_ref[...]) * (dq ** -0.5 * LOG2E)
    lane = lax.broadcasted_iota(jnp.int32, q.shape, 1)
    dn = (((1,), (1,)), ((), ()))
    kc = min(DA_KEY_CHUNK, s)

    def softmax_av(qm):
        m = jnp.full((qm.shape[0], 1), -jnp.inf, F32)
        acc = jnp.zeros((qm.shape[0], 2 * HEAD_DIM), F32)
        for c0 in range(0, s, kc):
            sc = lax.dot_general(qm, kr_ref[c0:c0 + kc, :], dn, preferred_element_type=F32)
            m_new = jnp.maximum(m, jnp.max(sc, axis=-1, keepdims=True))
            p = jnp.exp2(sc - m_new).astype(BF16)
            acc = jnp.exp2(m - m_new) * acc + jnp.dot(p, v1_ref[c0:c0 + kc, :],
                                                       preferred_element_type=F32)
            m = m_new
        return acc[:, :HEAD_DIM] / acc[:, HEAD_DIM:]

    o = (softmax_av(jnp.where(lane < dq, q, 0.0).astype(BF16))
         - lam_ref[0] * softmax_av(jnp.where(lane >= dq, q, 0.0).astype(BF16)))
    o_ref[...] = (_rms(o, g_ref[...]) * out_scale).astype(o_ref.dtype)


def differential_attention(proj, lam, subln, lambda_init):
    b, s, _ = proj.shape
    tq = min(DA_Q_BLOCK, s)
    base = 3 * NA_HEADS
    c, s_first, s_second = _rope_tables(s)
    qtab = pl.BlockSpec((tq, HEAD_DIM), lambda bi, h, i: (i, 0))
    ktab = pl.BlockSpec((s, HEAD_DIM), lambda bi, h, i: (0, 0))
    return pl.pallas_call(
        functools.partial(_da_kernel, out_scale=1.0 - lambda_init),
        grid=(b, DA_HEADS, s // tq),
        in_specs=[
            pl.BlockSpec(memory_space=pltpu.SMEM),
            pl.BlockSpec((None, tq, HEAD_DIM), lambda bi, h, i: (bi, i, base + h)),
            pl.BlockSpec((None, s, HEAD_DIM), lambda bi, h, i: (bi, 0, base + DA_HEADS + h)),
            pl.BlockSpec((None, s, HEAD_DIM), lambda bi, h, i: (bi, 0, base + 2 * DA_HEADS + h)),
            qtab, qtab, qtab, ktab, ktab, ktab,
            pl.BlockSpec((1, HEAD_DIM), lambda bi, h, i: (0, 0)),
        ],
        out_specs=pl.BlockSpec((None, tq, HEAD_DIM), lambda bi, h, i: (bi, i, h)),
        out_shape=jax.ShapeDtypeStruct((b, s, DA_HEADS * HEAD_DIM), BF16),
        scratch_shapes=[pltpu.VMEM((s, HEAD_DIM), BF16), pltpu.VMEM((s, 2 * HEAD_DIM), BF16)],
        compiler_params=_cparams(("arbitrary", "arbitrary", "arbitrary")),
        name="da_attention",
    )(lam, proj, proj, proj, c, s_first, s_second, c, s_first, s_second, subln.reshape(1, HEAD_DIM))


SCAN_PAD_ROWS = 8


def _shift_rows(x, k):
    return pltpu.roll(x, k % x.shape[0], axis=0)


def _lru_kernel(gate_ref, rec_ref, cw_ref, cb_ref, wa_ref, ba_ref, wx_ref, bx_ref, ap_ref, o_ref,
                u_ref, af_ref, bf_ref, ab_ref, bb_ref):
    s = rec_ref.shape[0]
    seg = s // SUBLANES
    pitch = seg + SCAN_PAD_ROWS

    r = rec_ref[...].astype(F32)
    t_idx = lax.broadcasted_iota(jnp.int32, r.shape, 0)
    cw = cw_ref[...]
    u = r * cw[1:2, :] + cb_ref[...]
    u = u + jnp.where(t_idx >= 1, _shift_rows(r, 1), 0.0) * cw[0:1, :]
    u = u + jnp.where(t_idx < s - 1, _shift_rows(r, -1), 0.0) * cw[2:3, :]
    u = u + jnp.where(t_idx < s - 2, _shift_rows(r, -2), 0.0) * cw[3:4, :]
    u_ref[...] = u

    for d, (a_ref, b_ref) in enumerate(((af_ref, bf_ref), (ab_ref, bb_ref))):
        ap = ap_ref[d]
        softplus = jnp.maximum(-ap, 0.0) + jnp.log1p(jnp.exp(-jnp.abs(ap)))
        wa = wa_ref[d].astype(BF16)
        wx = wx_ref[d].astype(BF16)

        def prep(c, carry, d=d, a_ref=a_ref, b_ref=b_ref, softplus=softplus, wa=wa, wx=wx):
            uc = u_ref[pl.ds(pl.multiple_of(c * seg, seg), seg), :]
            ub = uc.astype(BF16)
            ga = _sigmoid(jnp.dot(ub, wa, preferred_element_type=F32) + ba_ref[d])
            gx = _sigmoid(jnp.dot(ub, wx, preferred_element_type=F32) + bx_ref[d])
            a = jnp.exp(-LRU_C * ga * softplus)
            r0 = pl.multiple_of(c * pitch, SUBLANES)
            a_ref[pl.ds(r0, seg), :] = a
            b_ref[pl.ds(r0, seg), :] = jnp.sqrt(1.0 - a * a) * (gx * uc)
            return carry

        lax.fori_loop(0, SUBLANES, prep, 0)

    def step(i, carry):
        hf, pf, hb, pb = carry
        rows_f = pl.ds(i, SUBLANES, stride=pitch)
        rows_b = pl.ds(seg - 1 - i, SUBLANES, stride=pitch)
        a_f = af_ref[rows_f, :]
        a_b = ab_ref[rows_b, :]
        hf = a_f * hf + bf_ref[rows_f, :]
        hb = a_b * hb + bb_ref[rows_b, :]
        pf = a_f * pf
        pb = a_b * pb
        bf_ref[rows_f, :] = hf
        af_ref[rows_f, :] = pf
        bb_ref[rows_b, :] = hb
        ab_ref[rows_b, :] = pb
        return hf, pf, hb, pb

    zero = jnp.zeros((SUBLANES, LRU_BLOCK_DIM), F32)
    one = jnp.ones((SUBLANES, LRU_BLOCK_DIM), F32)
    hf, pf, hb, pb = lax.fori_loop(0, seg, step, (zero, one, zero, one), unroll=4)

    cin_f = [jnp.zeros((1, LRU_BLOCK_DIM), F32)]
    for c in range(SUBLANES - 1):
        cin_f.append(pf[c:c + 1] * cin_f[-1] + hf[c:c + 1])
    cin_b = [jnp.zeros((1, LRU_BLOCK_DIM), F32)]
    for c in range(SUBLANES - 1, 0, -1):
        cin_b.insert(0, pb[c:c + 1] * cin_b[0] + hb[c:c + 1])

    for c in range(SUBLANES):
        rows = pl.ds(c * pitch, seg)
        h = (bf_ref[rows, :] + af_ref[rows, :] * cin_f[c]) + (bb_ref[rows, :] + ab_ref[rows, :] * cin_b[c])
        y = jax.nn.gelu(gate_ref[pl.ds(c * seg, seg), :].astype(F32), approximate=True)
        o_ref[pl.ds(c * seg, seg), :] = (y * h).astype(o_ref.dtype)


def recurrent_block(proj, conv_w, conv_b, w_a, b_a, w_x, b_x, a_param):
    b, s, w2 = proj.shape
    w = w2 // 2
    nb = w // LRU_BLOCK_DIM
    c = LRU_BLOCK_DIM
    vec = lambda p: p.reshape(2, 1, w)
    return pl.pallas_call(
        _lru_kernel,
        grid=(b, nb),
        in_specs=[
            pl.BlockSpec((None, s, c), lambda bi, n: (bi, 0, n)),
            pl.BlockSpec((None, s, c), lambda bi, n: (bi, 0, nb + n)),
            pl.BlockSpec((conv_w.shape[0], c), lambda bi, n: (0, n)),
            pl.BlockSpec((1, c), lambda bi, n: (0, n)),
            pl.BlockSpec((2, None, c, c), lambda bi, n: (0, n, 0, 0)),
            pl.BlockSpec((2, 1, c), lambda bi, n: (0, 0, n)),
            pl.BlockSpec((2, None, c, c), lambda bi, n: (0, n, 0, 0)),
            pl.BlockSpec((2, 1, c), lambda bi, n: (0, 0, n)),
            pl.BlockSpec((2, 1, c), lambda bi, n: (0, 0, n)),
        ],
        out_specs=pl.BlockSpec((None, s, c), lambda bi, n: (bi, 0, n)),
        out_shape=jax.ShapeDtypeStruct((b, s, w), BF16),
        scratch_shapes=[pltpu.VMEM((s, c), F32)]
        + [pltpu.VMEM((s + SUBLANES * SCAN_PAD_ROWS, c), F32) for _ in range(4)],
        compiler_params=_cparams(("arbitrary", "arbitrary")),
        name="rg_lru",
    )(proj, proj, conv_w, conv_b.reshape(1, w), w_a, vec(b_a), w_x, vec(b_x), vec(a_param))


R_E0, R_E1, R_G0, R_G1, R_RANK0, R_RANK1 = range(6)


def _route_kernel(x_ref, g_ref, r_ref, meta_ref, cnt_ref, carry_ref):
    i = pl.program_id(0)

    @pl.when(i == 0)
    def _():
        carry_ref[...] = jnp.zeros_like(carry_ref)

    h = _rms(x_ref[...], g_ref[...])
    r = r_ref[...]
    h_hi = h.astype(BF16)
    r_hi = r.astype(BF16)
    h_lo = (h - h_hi.astype(F32)).astype(BF16)
    r_lo = (r - r_hi.astype(F32)).astype(BF16)
    logits = (jnp.dot(h_hi, r_hi, preferred_element_type=F32)
              + jnp.dot(h_lo, r_hi, preferred_element_type=F32)
              + jnp.dot(h_hi, r_lo, preferred_element_type=F32))
    tm = logits.shape[0]
    lane = lax.broadcasted_iota(jnp.int32, logits.shape, 1).astype(F32)
    logits = jnp.where(lane < N_EXPERTS, logits, -jnp.inf)
    m0 = jnp.max(logits, axis=-1, keepdims=True)
    e0 = jnp.min(jnp.where(logits == m0, lane, float(LANES)), axis=-1, keepdims=True)
    rest = jnp.where(lane == e0, -jnp.inf, logits)
    m1 = jnp.max(rest, axis=-1, keepdims=True)
    e1 = jnp.min(jnp.where(rest == m1, lane, float(LANES)), axis=-1, keepdims=True)
    z = jnp.exp(m1 - m0)
    g0 = 1.0 / (1.0 + z)
    g1 = z / (1.0 + z)

    hit0 = lane == e0
    hit1 = lane == e1
    both = jnp.where(hit0 | hit1, 1.0, 0.0).astype(BF16)
    ri = lax.broadcasted_iota(jnp.int32, (tm, tm), 0)
    ci = lax.broadcasted_iota(jnp.int32, (tm, tm), 1)
    lower = jnp.where(ci < ri, 1.0, 0.0).astype(BF16)
    before = jnp.dot(lower, both, preferred_element_type=F32) + carry_ref[0:1, :]
    rank0 = jnp.sum(jnp.where(hit0, before, 0.0), axis=-1, keepdims=True)
    rank1 = jnp.sum(jnp.where(hit1, before, 0.0), axis=-1, keepdims=True)
    total = carry_ref[0:1, :] + jnp.sum(both.astype(F32), axis=0, keepdims=True)
    carry_ref[...] = jnp.broadcast_to(total, carry_ref.shape)
    cnt_ref[...] = jnp.broadcast_to(total, cnt_ref.shape)

    rec = jnp.zeros(logits.shape, F32)
    for col, val in ((R_E0, e0), (R_E1, e1), (R_G0, g0), (R_G1, g1),
                     (R_RANK0, rank0), (R_RANK1, rank1)):
        rec = jnp.where(lane == col, val, rec)
    meta_ref[...] = rec


def route(x, g, router):
    t, d = x.shape
    tm = ROUTE_TILE
    r_pad = jnp.zeros((d, LANES), F32).at[:, :N_EXPERTS].set(router)
    return pl.pallas_call(
        _route_kernel,
        grid=(t // tm,),
        in_specs=[pl.BlockSpec((tm, d), lambda i: (i, 0)),
                  pl.BlockSpec((1, d), lambda i: (0, 0)),
                  pl.BlockSpec((d, LANES), lambda i: (0, 0))],
        out_specs=[pl.BlockSpec((tm, LANES), lambda i: (i, 0)),
                   pl.BlockSpec((SUBLANES, LANES), lambda i: (0, 0))],
        out_shape=[jax.ShapeDtypeStruct((t, LANES), F32),
                   jax.ShapeDtypeStruct((SUBLANES, LANES), F32)],
        scratch_shapes=[pltpu.VMEM((SUBLANES, LANES), F32)],
        compiler_params=_cparams(("arbitrary",)),
        name="moe_route",
    )(x, g.reshape(1, d), r_pad)


def _row_copy(src_ref, src_row, dst_ref, dst_row, sem):
    return pltpu.make_async_copy(src_ref.at[pl.ds(src_row, 1), :], dst_ref.at[pl.ds(dst_row, 1), :], sem)


def _tile_wait(src_ref, tile_ref, sem):
    rows = tile_ref.shape[0]
    pltpu.make_async_copy(src_ref.at[pl.ds(0, rows), :], tile_ref, sem).wait()


def _dispatch_kernel(p0_ref, p1_ref, gs_ref, gt_ref, x_ref, g_ref, xs_ref, h_ref, sems, *, n_tiles):
    tm = x_ref.shape[0]
    base = pl.program_id(0) * tm

    @pl.when(pl.program_id(0) == 0)
    def _():
        h_ref[...] = jnp.zeros_like(h_ref)

        def zero_tile(tile):
            row = pl.multiple_of(tile * tm, tm)
            return pltpu.make_async_copy(h_ref, xs_ref.at[pl.ds(row, tm), :], sems.at[0])

        n_groups = gs_ref.shape[0]
        for e in range(n_groups):
            @pl.when(gt_ref[e] > 0)
            def _(e=e):
                zero_tile(gs_ref[e] + gt_ref[e] - 1).start()

        used = gs_ref[n_groups - 1] + gt_ref[n_groups - 1]

        def tail(tile, carry):
            zero_tile(tile).start()
            return carry

        lax.fori_loop(used, n_tiles, tail, 0)
        for e in range(n_groups):
            @pl.when(gt_ref[e] > 0)
            def _():
                zero_tile(0).wait()

        def tail_wait(tile, carry):
            zero_tile(0).wait()
            return carry

        lax.fori_loop(used, n_tiles, tail_wait, 0)

    h_ref[...] = _rms(x_ref[...], g_ref[...])

    def start(t, carry):
        _row_copy(h_ref, t, xs_ref, p0_ref[base + t], sems.at[0]).start()
        _row_copy(h_ref, t, xs_ref, p1_ref[base + t], sems.at[1]).start(priority=1)
        return carry

    lax.fori_loop(0, tm, start, 0, unroll=ROW_COPY_UNROLL)
    for k in range(2):
        _tile_wait(xs_ref, h_ref, sems.at[k])


def dispatch(x, g, pos0, pos1, group_start, group_tiles, n_rows):
    t, d = x.shape
    tm = ROUTE_TILE
    grid_spec = pltpu.PrefetchScalarGridSpec(
        num_scalar_prefetch=4,
        grid=(t // tm,),
        in_specs=[pl.BlockSpec((tm, d), lambda i, *_: (i, 0)),
                  pl.BlockSpec((1, d), lambda i, *_: (0, 0))],
        out_specs=pl.BlockSpec(memory_space=pl.ANY),
        scratch_shapes=[pltpu.VMEM((tm, d), F32), pltpu.SemaphoreType.DMA((2,))],
    )
    return pl.pallas_call(
        functools.partial(_dispatch_kernel, n_tiles=n_rows // tm),
        grid_spec=grid_spec,
        out_shape=jax.ShapeDtypeStruct((n_rows, d), F32),
        compiler_params=_cparams(("arbitrary",)),
        name="moe_dispatch",
    )(pos0, pos1, group_start, group_tiles, x, g.reshape(1, d))


def _combine_kernel(p0_ref, p1_ref, x_ref, meta_ref, g_ref, y_ref, o_ref, y0_ref, y1_ref, sems):
    tm = x_ref.shape[0]
    base = pl.program_id(0) * tm

    def start(t, carry):
        _row_copy(y_ref, p0_ref[base + t], y0_ref, t, sems.at[0]).start()
        _row_copy(y_ref, p1_ref[base + t], y1_ref, t, sems.at[1]).start(priority=1)
        return carry

    lax.fori_loop(0, tm, start, 0, unroll=ROW_COPY_UNROLL)
    _tile_wait(y_ref, y0_ref, sems.at[0])
    _tile_wait(y_ref, y1_ref, sems.at[1])
    meta = meta_ref[...]
    g0 = meta[:, R_G0:R_G0 + 1]
    g1 = meta[:, R_G1:R_G1 + 1]
    out = x_ref[...] + g0 * y0_ref[...] + g1 * y1_ref[...]
    o_ref[...] = _rms(out, g_ref[...])


def combine(x, meta, g, y, pos0, pos1):
    t, d = x.shape
    tm = ROUTE_TILE
    grid_spec = pltpu.PrefetchScalarGridSpec(
        num_scalar_prefetch=2,
        grid=(t // tm,),
        in_specs=[pl.BlockSpec((tm, d), lambda i, p0, p1: (i, 0)),
                  pl.BlockSpec((tm, LANES), lambda i, p0, p1: (i, 0)),
                  pl.BlockSpec((1, d), lambda i, p0, p1: (0, 0)),
                  pl.BlockSpec(memory_space=pl.ANY)],
        out_specs=pl.BlockSpec((tm, d), lambda i, p0, p1: (i, 0)),
        scratch_shapes=[pltpu.VMEM((tm, d), F32), pltpu.VMEM((tm, d), F32),
                        pltpu.SemaphoreType.DMA((2,))],
    )
    return pl.pallas_call(
        _combine_kernel,
        grid_spec=grid_spec,
        out_shape=jax.ShapeDtypeStruct((t, d), F32),
        compiler_params=_cparams(("arbitrary",)),
        name="moe_combine",
    )(pos0, pos1, x, meta, g.reshape(1, d), y)


def _moe_plan(meta, counts, tm):
    counts = counts[0, :N_EXPERTS].astype(jnp.int32)
    group_tiles = (counts + tm - 1) // tm
    group_start = jnp.cumsum(group_tiles) - group_tiles
    e0 = meta[:, R_E0].astype(jnp.int32)
    e1 = meta[:, R_E1].astype(jnp.int32)
    pos0 = group_start[e0] * tm + meta[:, R_RANK0].astype(jnp.int32)
    pos1 = group_start[e1] * tm + meta[:, R_RANK1].astype(jnp.int32)
    return pos0, pos1, group_start, group_tiles


def _moe_chunks(group_start, group_tiles, chunk_tiles, n_chunks):
    per = (group_tiles + chunk_tiles - 1) // chunk_tiles
    c_end = jnp.cumsum(per)
    c_start = c_end - per
    total = c_end[-1]
    c = jnp.arange(n_chunks, dtype=jnp.int32)
    c_eff = jnp.minimum(c, jnp.maximum(total - 1, 0))
    expert = jnp.sum((c_eff[:, None] >= c_end[None, :]).astype(jnp.int32), axis=1)
    expert = jnp.minimum(expert, N_EXPERTS - 1)
    k = c_eff - c_start[expert]
    first_tile = group_start[expert] + k * chunk_tiles
    tiles = jnp.clip(group_tiles[expert] - k * chunk_tiles, 0, chunk_tiles)
    tiles = jnp.where(c < total, tiles, 0)
    used = (group_start[-1] + group_tiles[-1]).reshape(1)
    return expert, first_tile, tiles, used


def kernel(x, ev_mix_norm, ev_w_in, ev_na_rpb, ev_da_lambda_q1, ev_da_lambda_k1, ev_da_lambda_q2, ev_da_lambda_k2, ev_da_subln, ev_w_out, ev_ffn_norm, ev_ffn_w_gate, ev_ffn_w_up, ev_ffn_w_down, od_mix_norm, od_w_in, od_conv_w, od_conv_b, od_lru_w_a, od_lru_b_a, od_lru_w_x, od_lru_b_x, od_lru_a_param, od_w_out, od_ffn_norm, od_router, od_moe_w_gate, od_moe_w_up, od_moe_w_down, final_norm):
    b, s, d = x.shape
    t = b * s
    xt = x.reshape(t, d)

    lambda_init = 0.8 - 0.6 * math.exp(-0.3 * 0)
    h = rmsnorm(xt, ev_mix_norm[0], BF16)
    proj = ws_matmul(h, [ev_w_in], tm=512, tn=1024, epilogue="cast", out_dtype=BF16, name="l0_in_proj")
    proj = proj.reshape(b, s, -1)
    na_out = neighbourhood_attention(proj, ev_na_rpb[0])
    lam = (jnp.exp(jnp.sum(ev_da_lambda_q1[0] * ev_da_lambda_k1[0]))
           - jnp.exp(jnp.sum(ev_da_lambda_q2[0] * ev_da_lambda_k2[0])) + lambda_init).reshape(1)
    da_out = differential_attention(proj, lam, ev_da_subln[0], lambda_init)
    mix = jnp.concatenate([na_out, da_out], axis=-1).reshape(t, -1)
    xt = ws_matmul(mix, [ev_w_out], tm=512, tn=1024, epilogue="residual", out_dtype=F32, res=xt,
                   name="l0_out_proj")

    h = rmsnorm(xt, ev_ffn_norm[0], BF16)
    act = ws_matmul(h, [ev_ffn_w_gate, ev_ffn_w_up], tm=512, tn=1024, epilogue="swiglu",
                    out_dtype=BF16, name="l0_ffn_up")
    xt = ws_matmul(act, [ev_ffn_w_down], tm=256, tn=512, epilogue="residual", out_dtype=F32, res=xt,
                   name="l0_ffn_down")

    h = rmsnorm(xt, od_mix_norm[0], BF16)
    proj = ws_matmul(h, [od_w_in], tm=512, tn=1024, epilogue="cast", out_dtype=BF16, name="l1_in_proj")
    rec = recurrent_block(proj.reshape(b, s, -1), od_conv_w[0], od_conv_b[0], od_lru_w_a[0],
                          od_lru_b_a[0], od_lru_w_x[0], od_lru_b_x[0], od_lru_a_param[0])
    xt = ws_matmul(rec.reshape(t, -1), [od_w_out], tm=512, tn=1024, epilogue="residual",
                   out_dtype=F32, res=xt, name="l1_out_proj")

    tm = MOE_ROW_TILE
    n_tiles = (t * 2) // tm + N_EXPERTS
    meta, counts = route(xt, od_ffn_norm[0], od_router[0])
    pos0, pos1, group_start, group_tiles = _moe_plan(meta, counts, tm)
    xs = dispatch(xt, od_ffn_norm[0], pos0, pos1, group_start, group_tiles, n_tiles * tm)
    n_chunks = (n_tiles - N_EXPERTS) // MOE_CHUNK_TILES + N_EXPERTS
    chunks = _moe_chunks(group_start, group_tiles, MOE_CHUNK_TILES, n_chunks)
    y = fused_expert_swiglu(xs, od_moe_w_gate[0], od_moe_w_up[0], od_moe_w_down[0], chunks, tm=tm,
                            chunk_tiles=MOE_CHUNK_TILES, tf=FFN_COL_TILE, name="moe_ffn")
    out = combine(xt, meta, final_norm, y, pos0, pos1)
    return out.reshape(b, s, d)
```
